```python
import math
import jax, jax.numpy as jnp
from jax import lax
import numpy as np

D_MODEL = 1024
BATCH = 8
SEQ = 2048
DEPTH = 1
DEC_BATCH = 32
DEC_SEQ = 8
PAST_LEN = 8192
PAGE_SIZE = 128

HGRN_HEADS = 8
HGRN_DK = 128
HGRN_DV = D_MODEL // HGRN_HEADS
HGRN_CHUNK = 64
MOBA_HEADS = 8
MOBA_DH = D_MODEL // MOBA_HEADS
MOBA_BLOCK = 256
MOBA_TOPK = 3
MOBA_Q_CHUNK = 16
D_FF = 2816
CONV_W = 3
NORM_EPS = 1e-6
IN_SPLITS = (HGRN_HEADS * HGRN_DK, HGRN_HEADS * HGRN_DK, HGRN_HEADS * HGRN_DV, HGRN_HEADS * HGRN_DV,
             MOBA_HEADS * MOBA_DH, MOBA_HEADS * MOBA_DH, MOBA_HEADS * MOBA_DH, D_MODEL, D_MODEL)
D_IN = sum(IN_SPLITS)

kernel_name = 'hgrn2_moba_convffn_hybrid_step'


def rmsnorm(x, gain):
    xf = x.astype(jnp.float32)
    y = xf * lax.rsqrt(jnp.mean(xf * xf, axis=-1, keepdims=True) + NORM_EPS)
    return (y * gain.astype(jnp.float32)).astype(x.dtype)


def hgrn2_chunked(q, k, v, log_f, s0):
    B, H, T, DK = q.shape
    DV = v.shape[-1]
    C = math.gcd(T, HGRN_CHUNK)
    n = T // C

    def to_chunks(a):
        return jnp.moveaxis(a.reshape(B, H, n, C, a.shape[-1]), 2, 0)

    causal = jnp.tril(jnp.ones((C, C), dtype=bool))

    def step(S, xs):
        qc, kc, vc, gc = xs
        b = jnp.cumsum(gc, axis=2)
        inter = jnp.einsum('bhtk,bhkv->bhtv', qc * jnp.exp(b), S)
        diff = b[:, :, :, None, :] - b[:, :, None, :, :]
        decay = jnp.exp(jnp.where(causal[None, None, :, :, None], diff, -jnp.inf))
        scores = jnp.einsum('bhtk,bhsk,bhtsk->bhts', qc, kc, decay)
        intra = jnp.einsum('bhts,bhsv->bhtv', scores, vc)
        b_last = b[:, :, -1:, :]
        S_new = jnp.exp(b_last[:, :, 0, :, None]) * S + jnp.einsum('bhsk,bhsv->bhkv', kc * jnp.exp(b_last - b), vc)
        return S_new, inter + intra

    S, o = lax.scan(step, s0, (to_chunks(q), to_chunks(k), to_chunks(v), to_chunks(log_f)))
    o = jnp.moveaxis(o, 0, 2).reshape(B, H, T, DV)
    return o, S


def moba_attention(q, k_parts, v_parts, q_pos):
    B, T, H, d = q.shape
    L = sum(p.shape[1] for p in k_parts)
    NB = -(-L // MOBA_BLOCK)
    pad = NB * MOBA_BLOCK - L
    zeros = jnp.zeros((B, pad, H, d), dtype=q.dtype)
    k = jnp.concatenate(list(k_parts) + [zeros], axis=1).reshape(B, NB, MOBA_BLOCK, H, d)
    v = jnp.concatenate(list(v_parts) + [zeros], axis=1).reshape(B, NB, MOBA_BLOCK, H, d)
    k_mean = jnp.mean(k.astype(jnp.float32), axis=2)
    q_blk = q_pos // MOBA_BLOCK
    s = jnp.einsum('bthd,bnhd->bhtn', q.astype(jnp.float32), k_mean)
    past = jnp.arange(NB, dtype=jnp.int32)[None, :] < q_blk[:, None]
    s = jnp.where(past[None, None], s, -jnp.inf)
    kk = min(MOBA_TOPK, NB)
    _, sel = lax.top_k(s, kk)
    sel_valid = sel < q_blk[None, None, :, None]
    own = jnp.broadcast_to(q_blk[None, None, :, None], (B, H, T, 1)).astype(sel.dtype)
    blocks = jnp.concatenate([sel, own], axis=-1).transpose(0, 2, 1, 3)
    valid = jnp.concatenate([sel_valid, jnp.ones((B, H, T, 1), dtype=bool)], axis=-1).transpose(0, 2, 1, 3)
    K = kk + 1
    QC = math.gcd(T, MOBA_Q_CHUNK)
    nq = T // QC
    xs = (jnp.repeat(jnp.arange(B, dtype=jnp.int32), nq),
          q.reshape(B * nq, QC, H, d),
          blocks.reshape(B * nq, QC, H, K),
          valid.reshape(B * nq, QC, H, K),
          jnp.tile(q_pos.reshape(nq, QC), (B, 1)))
    scale = 1.0 / math.sqrt(d)
    offs = jnp.arange(MOBA_BLOCK, dtype=jnp.int32)
    hidx = jnp.arange(H, dtype=jnp.int32)[None, :, None]

    def attend(item):
        bi, qi, blk, val, pos = item
        kg = k[bi][blk, :, hidx]
        vg = v[bi][blk, :, hidx]
        logits = jnp.einsum('qhd,qhksd->qhks', qi.astype(jnp.float32), kg.astype(jnp.float32)) * scale
        kpos = blk[..., None] * MOBA_BLOCK + offs
        mask = val[..., None] & (kpos <= pos[:, None, None, None])
        logits = jnp.where(mask, logits, -jnp.inf)
        p = jax.nn.softmax(logits.reshape(QC, H, -1), axis=-1).reshape(logits.shape)
        return jnp.einsum('qhks,qhksd->qhd', p.astype(vg.dtype), vg)

    out = lax.map(attend, xs)
    return out.reshape(B, T, H * d)


def token_mixers(h, kv_past, q_pos, s0, lb, w_in, hgrn_gain, w_out):
    B, T, _ = h.shape
    offsets = [int(o) for o in np.cumsum(IN_SPLITS)[:-1]]
    hq, hf, hi, hg, aq, ak, av, ga, gb = jnp.split(h @ w_in, offsets, axis=-1)
    q = jax.nn.silu(hq.astype(jnp.float32)).reshape(B, T, HGRN_HEADS, HGRN_DK)
    f = lb.reshape(HGRN_HEADS, HGRN_DK) + (1.0 - lb.reshape(HGRN_HEADS, HGRN_DK)) * jax.nn.sigmoid(
        hf.astype(jnp.float32).reshape(B, T, HGRN_HEADS, HGRN_DK))
    i = hi.astype(jnp.float32).reshape(B, T, HGRN_HEADS, HGRN_DV)
    tr = lambda a: a.transpose(0, 2, 1, 3)
    o, S = hgrn2_chunked(tr(q), tr(1.0 - f), tr(i), tr(jnp.log(f)), s0.astype(jnp.float32))
    o = rmsnorm(tr(o), hgrn_gain.reshape(HGRN_HEADS, HGRN_DV)).reshape(B, T, D_MODEL).astype(h.dtype)
    branch_a = o * jax.nn.silu(hg)
    qa = aq.reshape(B, T, MOBA_HEADS, MOBA_DH)
    k_new = ak.reshape(B, T, MOBA_HEADS, MOBA_DH)
    v_new = av.reshape(B, T, MOBA_HEADS, MOBA_DH)
    if kv_past is None:
        k_parts, v_parts = [k_new], [v_new]
    else:
        k_parts, v_parts = [kv_past[0].astype(h.dtype), k_new], [kv_past[1].astype(h.dtype), v_new]
    branch_b = moba_attention(qa, k_parts, v_parts, q_pos)
    merged = jax.nn.sigmoid(ga) * branch_a + jax.nn.sigmoid(gb) * branch_b
    return merged @ w_out, k_new, v_new, S


def conv_ffn(h, buf, w_up, conv_w, conv_b, w_down):
    T = h.shape[1]
    up = h @ w_up
    ext = jnp.concatenate([buf.astype(up.dtype), up], axis=1)
    conv = conv_b
    for j in range(CONV_W):
        conv = conv + conv_w[j] * ext[:, j:j + T]
    u_gate, u_lin = jnp.split(conv, 2, axis=-1)
    out = (jax.nn.gelu(u_gate) * u_lin) @ w_down
    return out, ext[:, -(CONV_W - 1):]


def trunk_layer(x, kv_past, q_pos, s0, ffn_buf, lb, n_mix_pre, n_mix_post, w_in, hgrn_gain, w_out,
                n_ffn_pre, n_ffn_post, w_up, conv_w, conv_b, w_down):
    m, k_new, v_new, S = token_mixers(rmsnorm(x, n_mix_pre), kv_past, q_pos, s0, lb, w_in, hgrn_gain, w_out)
    x = x + rmsnorm(m, n_mix_post)
    f, new_buf = conv_ffn(rmsnorm(x, n_ffn_pre), ffn_buf, w_up, conv_w, conv_b, w_down)
    x = x + rmsnorm(f, n_ffn_post)
    return x, k_new, v_new, S, new_buf


def setup_inputs(seed: int = 0) -> dict:
    key = jax.random.key(seed)
    ks = jax.random.split(key, 20)
    n_pages = PAST_LEN // PAGE_SIZE
    n_used = DEC_BATCH * n_pages
    n_pool = n_used + max(1, n_used // 4)
    nrm = jax.random.normal
    page_table = jax.random.permutation(ks[0], n_pool)[:n_used].reshape(DEC_BATCH, n_pages).astype(jnp.int32)
    return {
        'x_prompt': nrm(ks[1], (BATCH, SEQ, D_MODEL), jnp.float32),
        'x_sample': nrm(ks[2], (DEC_BATCH, DEC_SEQ, D_MODEL), jnp.float32),
        'cache_k': nrm(ks[3], (DEPTH, n_pool, PAGE_SIZE, MOBA_HEADS, MOBA_DH), jnp.float32),
        'cache_v': nrm(ks[4], (DEPTH, n_pool, PAGE_SIZE, MOBA_HEADS, MOBA_DH), jnp.float32),
        'page_table': page_table,
        'state_hgrn': 0.5 * nrm(ks[5], (DEPTH, DEC_BATCH, HGRN_HEADS, HGRN_DK, HGRN_DV), jnp.float32),
        'state_ffn_conv': nrm(ks[6], (DEPTH, DEC_BATCH, CONV_W - 1, 2 * D_FF), jnp.float32),
        'norm_mix_pre': 1.0 + 0.01 * nrm(ks[7], (DEPTH, D_MODEL), jnp.float32),
        'norm_mix_post': 1.0 + 0.01 * nrm(ks[8], (DEPTH, D_MODEL), jnp.float32),
        'w_in': nrm(ks[9], (DEPTH, D_MODEL, D_IN), jnp.float32) * D_MODEL ** -0.5,
        'hgrn_lb_logits': 0.1 * nrm(ks[10], (DEPTH + 1, HGRN_HEADS * HGRN_DK), jnp.float32),
        'hgrn_norm': 1.0 + 0.01 * nrm(ks[11], (DEPTH, HGRN_HEADS * HGRN_DV), jnp.float32),
        'w_out': nrm(ks[12], (DEPTH, D_MODEL, D_MODEL), jnp.float32) * D_MODEL ** -0.5,
        'norm_ffn_pre': 1.0 + 0.01 * nrm(ks[13], (DEPTH, D_MODEL), jnp.float32),
        'norm_ffn_post': 1.0 + 0.01 * nrm(ks[14], (DEPTH, D_MODEL), jnp.float32),
        'w_up': nrm(ks[15], (DEPTH, D_MODEL, 2 * D_FF), jnp.float32) * D_MODEL ** -0.5,
        'conv_w': nrm(ks[16], (DEPTH, CONV_W, 2 * D_FF), jnp.float32) * CONV_W ** -0.5,
        'conv_b': 0.01 * nrm(ks[17], (DEPTH, 2 * D_FF), jnp.float32),
        'w_down': nrm(ks[18], (DEPTH, D_FF, D_MODEL), jnp.float32) * D_FF ** -0.5,
    }


def reference(x_prompt, x_sample, cache_k, cache_v, page_table, state_hgrn, state_ffn_conv,
              norm_mix_pre, norm_mix_post, w_in, hgrn_lb_logits, hgrn_norm, w_out,
              norm_ffn_pre, norm_ffn_post, w_up, conv_w, conv_b, w_down):
    B, T = x_prompt.shape[0], x_prompt.shape[1]
    DB, DT = x_sample.shape[0], x_sample.shape[1]
    past_len = page_table.shape[1] * cache_k.shape[2]
    lb_all = jnp.cumsum(jax.nn.softmax(hgrn_lb_logits.astype(jnp.float32), axis=0), axis=0)
    pos_prompt = jnp.arange(T, dtype=jnp.int32)
    pos_sample = past_len + jnp.arange(DT, dtype=jnp.int32)
    yp, ys = x_prompt, x_sample
    kp_l, vp_l, sp_l, cp_l, ks_l, vs_l, ss_l, cs_l = [], [], [], [], [], [], [], []
    for l in range(DEPTH):
        w = (lb_all[l], norm_mix_pre[l], norm_mix_post[l], w_in[l], hgrn_norm[l], w_out[l],
             norm_ffn_pre[l], norm_ffn_post[l], w_up[l], conv_w[l], conv_b[l], w_down[l])
        s0p = jnp.zeros((B, HGRN_HEADS, HGRN_DK, HGRN_DV), dtype=jnp.float32)
        bufp = jnp.zeros((B, CONV_W - 1, 2 * D_FF), dtype=x_prompt.dtype)
        yp, kp, vp, sp, cp = trunk_layer(yp, None, pos_prompt, s0p, bufp, *w)
        past_k = cache_k[l, page_table].reshape(DB, past_len, MOBA_HEADS, MOBA_DH)
        past_v = cache_v[l, page_table].reshape(DB, past_len, MOBA_HEADS, MOBA_DH)
        ys, kn, vn, sn, cn = trunk_layer(ys, (past_k, past_v), pos_sample, state_hgrn[l], state_ffn_conv[l], *w)
        kp_l.append(kp); vp_l.append(vp); sp_l.append(sp); cp_l.append(cp)
        ks_l.append(kn); vs_l.append(vn); ss_l.append(sn); cs_l.append(cn)
    return (yp, ys, jnp.stack(kp_l), jnp.stack(vp_l), jnp.stack(sp_l), jnp.stack(cp_l),
            jnp.stack(ks_l), jnp.stack(vs_l), jnp.stack(ss_l), jnp.stack(cs_l))
```

```python
import functools
import math

import jax
import jax.numpy as jnp
from jax import lax
from jax.experimental import pallas as pl
from jax.experimental.pallas import tpu as pltpu

F32 = jnp.float32
BF16 = jnp.bfloat16

NORM_EPS = 1e-6
HEAD_DIM = 128
MOBA_BLOCK = 256
MOBA_TOPK = 3
HGRN_CHUNK = 64
CONV_W = 3
VMEM_LIMIT = 56 * 1024 * 1024


def _cparams(sem):
    return pltpu.CompilerParams(dimension_semantics=sem, vmem_limit_bytes=VMEM_LIMIT)


def _sigmoid(x):
    return 1.0 / (1.0 + jnp.exp(-x))


def _rms(x, gain):
    return x * lax.rsqrt(jnp.mean(x * x, axis=-1, keepdims=True) + NORM_EPS) * gain


def _dot(a, b):
    return jnp.dot(a, b, preferred_element_type=F32)


def _dot_nt(a, b):
    return lax.dot_general(a, b, (((1,), (1,)), ((), ())), preferred_element_type=F32)


def _dot_tn(a, b):
    return lax.dot_general(a, b, (((0,), (0,)), ((), ())), preferred_element_type=F32)


def _split_bf16(x, parts):
    out = []
    for _ in range(parts):
        p = x.astype(BF16)
        out.append(p)
        x = x - p.astype(F32)
    return out


def _dot_nt_split(a, b):
    ah, al = _split_bf16(a, 2)
    bh, bl = _split_bf16(b, 2)
    return _dot_nt(ah, bh) + _dot_nt(ah, bl) + _dot_nt(al, bh)


def _inproj_kernel(x_ref, g_ref, w_ref, p_ref, k_ref, v_ref, h_scr, *, n_main):
    j = pl.program_id(1)

    @pl.when(j == 0)
    def _():
        h_scr[...] = _rms(x_ref[...], g_ref[...]).astype(BF16)

    @pl.when(j < n_main)
    def _():
        p_ref[...] = _dot(h_scr[...], w_ref[...])

    @pl.when(j == n_main)
    def _():
        k_ref[...] = _dot(h_scr[...], w_ref[...])

    @pl.when(j == n_main + 1)
    def _():
        v_ref[...] = _dot(h_scr[...], w_ref[...])


def _inproj(x, gain, w_perm, tm):
    n, d = x.shape
    n_blocks = w_perm.shape[1] // d
    n_main = n_blocks - 2
    return pl.pallas_call(
        functools.partial(_inproj_kernel, n_main=n_main),
        grid=(n // tm, n_blocks),
        in_specs=[
            pl.BlockSpec((tm, d), lambda i, j: (i, 0)),
            pl.BlockSpec((1, d), lambda i, j: (0, 0)),
            pl.BlockSpec((d, d), lambda i, j: (0, j)),
        ],
        out_specs=[
            pl.BlockSpec((tm, d), lambda i, j: (i, jnp.minimum(j, n_main - 1))),
            pl.BlockSpec((tm, d), lambda i, j: (i, 0)),
            pl.BlockSpec((tm, d), lambda i, j: (i, 0)),
        ],
        out_shape=[
            jax.ShapeDtypeStruct((n, n_main * d), F32),
            jax.ShapeDtypeStruct((n, d), F32),
            jax.ShapeDtypeStruct((n, d), F32),
        ],
        scratch_shapes=[pltpu.VMEM((tm, d), BF16)],
        compiler_params=_cparams(("parallel", "arbitrary")),
        name="inproj",
    )(x, gain, w_perm)


def _hgrn_kernel(hq_ref, hf_ref, hi_ref, hg_ref, lb_ref, gain_ref, s0_ref, a_ref, sout_ref, st_scr,
                 *, chunk, n_chunks, heads):
    tt = pl.program_id(1)

    @pl.when(tt == 0)
    def _():
        for h in range(heads):
            st_scr[h] = s0_ref[0, h].T

    row = lax.broadcasted_iota(jnp.int32, (chunk, chunk), 0)
    col = lax.broadcasted_iota(jnp.int32, (chunk, chunk), 1)
    causal = row >= col
    mm_dtype = BF16 if chunk % 16 == 0 else F32
    mm = lambda a: a.astype(mm_dtype)
    tril = mm(jnp.where(causal, 1.0, 0.0))

    for h in range(heads):
        sl = slice(h * HEAD_DIM, (h + 1) * HEAD_DIM)
        lb = lb_ref[:, sl]
        gain = gain_ref[:, sl]
        st = st_scr[h]
        for c in range(n_chunks):
            rs = slice(c * chunk, (c + 1) * chunk)
            hq = hq_ref[rs, sl]
            q = hq * _sigmoid(hq)
            f = lb + (1.0 - lb) * _sigmoid(hf_ref[rs, sl])
            kk = 1.0 - f
            b = sum(_dot(tril, mm(part)) for part in _split_bf16(jnp.log(f), 3))
            b_last = b[chunk - 1:chunk, :]
            qt = mm(q * jnp.exp(b))
            kt = mm(kk * jnp.exp(-b))
            kh = mm(kk * jnp.exp(b_last - b))
            vb = mm(hi_ref[rs, sl])
            scores = jnp.where(causal, _dot_nt(qt, kt), 0.0)
            o = _dot(mm(scores), vb) + _dot_nt(qt, mm(st))
            st = st * jnp.exp(b_last) + _dot_tn(vb, kh)
            hg = hg_ref[rs, sl]
            a_ref[rs, sl] = _rms(o, gain) * (hg * _sigmoid(hg))
        st_scr[h] = st

    @pl.when(tt == pl.num_programs(1) - 1)
    def _():
        for h in range(heads):
            sout_ref[0, h] = st_scr[h].T


def _hgrn(proj, lb, gain, s0, batch, seq, t_tile, chunk):
    n = proj.shape[0]
    d = lb.shape[1]
    heads = d // HEAD_DIM
    n_t = seq // t_tile
    col_spec = lambda cb: pl.BlockSpec((t_tile, d), lambda b, t: (b * n_t + t, cb))
    state_spec = pl.BlockSpec((1, heads, HEAD_DIM, HEAD_DIM), lambda b, t: (b, 0, 0, 0))
    return pl.pallas_call(
        functools.partial(_hgrn_kernel, chunk=chunk, n_chunks=t_tile // chunk, heads=heads),
        grid=(batch, n_t),
        in_specs=[col_spec(0), col_spec(1), col_spec(2), col_spec(3),
                  pl.BlockSpec((1, d), lambda b, t: (0, 0)),
                  pl.BlockSpec((1, d), lambda b, t: (0, 0)),
                  state_spec],
        out_specs=[pl.BlockSpec((t_tile, d), lambda b, t: (b * n_t + t, 0)), state_spec],
        out_shape=[jax.ShapeDtypeStruct((n, d), F32),
                   jax.ShapeDtypeStruct((batch, heads, HEAD_DIM, HEAD_DIM), F32)],
        scratch_shapes=[pltpu.VMEM((heads, HEAD_DIM, HEAD_DIM), F32)],
        compiler_params=_cparams(("parallel", "arbitrary")),
        name="hgrn",
    )(proj, proj, proj, proj, lb, gain, s0)


def _moba_prompt_kernel(q_ref, k_ref, v_ref, o_ref, kmean_scr, m_scr, l_scr, acc_scr, *, nb, blk):
    j = pl.program_id(2)
    scale = 1.0 / math.sqrt(HEAD_DIM)

    @pl.when(j == 0)
    def _():
        for n in range(nb):
            kmean_scr[n:n + 1, :] = jnp.mean(k_ref[n * blk:(n + 1) * blk, :], axis=0, keepdims=True)

    q = q_ref[...]
    qb = q.astype(BF16)
    s = _dot_nt_split(q, kmean_scr[...])

    def selected(n):
        sn = s[:, n:n + 1]
        cnt = jnp.zeros((blk, 1), jnp.int32)
        for m in range(nb - 1):
            if m == n:
                continue
            sm = s[:, m:m + 1]
            beats = (sm >= sn) if m < n else (sm > sn)
            cnt = cnt + jnp.where(beats, jnp.where(m < j, 1, 0), 0)
        return cnt < MOBA_TOPK

    row = lax.broadcasted_iota(jnp.int32, (blk, blk), 0)
    col = lax.broadcasted_iota(jnp.int32, (blk, blk), 1)
    own = pl.multiple_of(j * blk, blk)
    logits = jnp.where(col <= row, _dot_nt(qb, k_ref[pl.ds(own, blk), :].astype(BF16)) * scale, -jnp.inf)
    m0 = jnp.max(logits, axis=1, keepdims=True)
    p = jnp.exp(logits - m0)
    m_scr[...] = m0
    l_scr[...] = jnp.sum(p, axis=1, keepdims=True)
    acc_scr[...] = _dot(p.astype(BF16), v_ref[pl.ds(own, blk), :].astype(BF16))

    for n in range(nb - 1):
        @pl.when(n < j)
        def _():
            ks = slice(n * blk, (n + 1) * blk)
            lg = jnp.where(selected(n), _dot_nt(qb, k_ref[ks, :].astype(BF16)) * scale, -jnp.inf)
            m_old = m_scr[...]
            m_new = jnp.maximum(m_old, jnp.max(lg, axis=1, keepdims=True))
            alpha = jnp.exp(m_old - m_new)
            pn = jnp.exp(lg - m_new)
            m_scr[...] = m_new
            l_scr[...] = alpha * l_scr[...] + jnp.sum(pn, axis=1, keepdims=True)
            acc_scr[...] = alpha * acc_scr[...] + _dot(pn.astype(BF16), v_ref[ks, :].astype(BF16))

    o_ref[...] = acc_scr[...] / l_scr[...]


def _moba_prompt(proj, k_new, v_new, batch, seq, q_col_block):
    n, d = k_new.shape
    heads = d // HEAD_DIM
    blk = MOBA_BLOCK
    nb = seq // blk
    return pl.pallas_call(
        functools.partial(_moba_prompt_kernel, nb=nb, blk=blk),
        grid=(batch, heads, nb),
        in_specs=[
            pl.BlockSpec((blk, HEAD_DIM), lambda b, h, j: (b * nb + j, q_col_block * heads + h)),
            pl.BlockSpec((seq, HEAD_DIM), lambda b, h, j: (b, h)),
            pl.BlockSpec((seq, HEAD_DIM), lambda b, h, j: (b, h)),
        ],
        out_specs=pl.BlockSpec((blk, HEAD_DIM), lambda b, h, j: (b * nb + j, h)),
        out_shape=jax.ShapeDtypeStruct((n, d), F32),
        scratch_shapes=[pltpu.VMEM((nb, HEAD_DIM), F32), pltpu.VMEM((blk, 1), F32),
                        pltpu.VMEM((blk, 1), F32), pltpu.VMEM((blk, HEAD_DIM), F32)],
        compiler_params=_cparams(("parallel", "parallel", "arbitrary")),
        name="moba_prompt",
    )(proj, k_new, v_new)


def _block_diag_queries(q, heads, rows):
    d = q.shape[1]
    tiled = jnp.concatenate([q] * heads, axis=0)
    r = lax.broadcasted_iota(jnp.int32, (heads * rows, d), 0) // rows
    c = lax.broadcasted_iota(jnp.int32, (heads * rows, d), 1) // HEAD_DIM
    return jnp.where(r == c, tiled, 0.0)


def _moba_sample_k_kernel(pt_ref, q_ref, k_ref, lg_ref, ksum_ref, qbd_scr, *, heads, rows):
    del pt_ref
    p = pl.program_id(1)

    @pl.when(p == 0)
    def _():
        qbd_scr[...] = _block_diag_queries(q_ref[...], heads, rows).astype(BF16)

    kp = k_ref[0]
    lg_ref[0] = _dot_nt(qbd_scr[...], kp.astype(BF16))
    ksum_ref[0, pl.ds(p, 1), :] = jnp.sum(kp, axis=0, keepdims=True)


def _moba_sample_v_kernel(pt_ref, lg_ref, ksum_ref, q_ref, kn_ref, vn_ref, v_ref, o_ref,
                          p_scr, acc_scr, l_scr, *, heads, rows, n_pages, page, pages_per_block):
    del pt_ref
    pg = pl.program_id(1)
    hr = heads * rows
    d = heads * HEAD_DIM
    scale = 1.0 / math.sqrt(HEAD_DIM)
    nb = n_pages // pages_per_block
    blk = page * pages_per_block

    @pl.when(pg == 0)
    def _():
        qbd = _block_diag_queries(q_ref[...], heads, rows)
        ksum = ksum_ref[0]
        kmean = jnp.concatenate(
            [sum(ksum[n * pages_per_block + i:n * pages_per_block + i + 1, :] for i in range(pages_per_block))
             for n in range(nb)], axis=0) * (1.0 / blk)
        s = _dot_nt_split(qbd, kmean)
        cnt = jnp.zeros((hr, nb), jnp.int32)
        idx = lax.broadcasted_iota(jnp.int32, (hr, nb), 1)
        for m in range(nb):
            sm = s[:, m:m + 1]
            beats = (sm > s) | ((sm == s) & (idx > m))
            cnt = cnt + jnp.where(beats, 1, 0)
        sel = cnt < MOBA_TOPK

        kn = jnp.concatenate([kn_ref[...], jnp.zeros((page - rows, d), F32)], axis=0)
        vn = jnp.concatenate([vn_ref[...], jnp.zeros((page - rows, d), F32)], axis=0)
        r_own = lax.broadcasted_iota(jnp.int32, (hr, page), 0) % rows
        c_own = lax.broadcasted_iota(jnp.int32, (hr, page), 1)
        lg_own = jnp.where(c_own <= r_own, _dot_nt(qbd.astype(BF16), kn.astype(BF16)) * scale, -jnp.inf)
        m = jnp.max(lg_own, axis=1, keepdims=True)
        for n in range(nb):
            lg = jnp.where(sel[:, n:n + 1], lg_ref[0, :, n * blk:(n + 1) * blk] * scale, -jnp.inf)
            m = jnp.maximum(m, jnp.max(lg, axis=1, keepdims=True))
        p_own = jnp.exp(lg_own - m)
        l = jnp.sum(p_own, axis=1, keepdims=True)
        for n in range(nb):
            pn = jnp.exp(jnp.where(sel[:, n:n + 1], lg_ref[0, :, n * blk:(n + 1) * blk] * scale, -jnp.inf) - m)
            l = l + jnp.sum(pn, axis=1, keepdims=True)
            for i in range(pages_per_block):
                p_scr[n * pages_per_block + i] = pn[:, i * page:(i + 1) * page].astype(BF16)
        l_scr[...] = l
        acc_scr[...] = _dot(p_own.astype(BF16), vn.astype(BF16))

    acc_scr[...] += _dot(p_scr[pg], v_ref[0].astype(BF16))

    @pl.when(pg == n_pages - 1)
    def _():
        out = acc_scr[...] / l_scr[...]
        for h in range(heads):
            o_ref[:, h * HEAD_DIM:(h + 1) * HEAD_DIM] = out[h * rows:(h + 1) * rows, h * HEAD_DIM:(h + 1) * HEAD_DIM]


def _moba_sample(proj, k_new, v_new, cache_k, cache_v, page_table, rows, q_col_block):
    n, d = k_new.shape
    heads = d // HEAD_DIM
    batch, n_pages = page_table.shape
    page = cache_k.shape[1]
    hr = heads * rows
    pt = page_table.reshape(-1)
    ppb = MOBA_BLOCK // page
    lg, ksum = pl.pallas_call(
        functools.partial(_moba_sample_k_kernel, heads=heads, rows=rows),
        grid_spec=pltpu.PrefetchScalarGridSpec(
            num_scalar_prefetch=1,
            grid=(batch, n_pages),
            in_specs=[
                pl.BlockSpec((rows, d), lambda b, p, pt: (b, q_col_block)),
                pl.BlockSpec((1, page, d), lambda b, p, pt: (pt[b * n_pages + p], 0, 0)),
            ],
            out_specs=[
                pl.BlockSpec((1, hr, page), lambda b, p, pt: (b, 0, p)),
                pl.BlockSpec((1, n_pages, d), lambda b, p, pt: (b, 0, 0)),
            ],
            scratch_shapes=[pltpu.VMEM((hr, d), BF16)],
        ),
        out_shape=[jax.ShapeDtypeStruct((batch, hr, n_pages * page), F32),
                   jax.ShapeDtypeStruct((batch, n_pages, d), F32)],
        compiler_params=_cparams(("parallel", "arbitrary")),
        name="moba_sample_k",
    )(pt, proj, cache_k)
    return pl.pallas_call(
        functools.partial(_moba_sample_v_kernel, heads=heads, rows=rows, n_pages=n_pages, page=page,
                          pages_per_block=ppb),
        grid_spec=pltpu.PrefetchScalarGridSpec(
            num_scalar_prefetch=1,
            grid=(batch, n_pages),
            in_specs=[
                pl.BlockSpec((1, hr, n_pages * page), lambda b, p, pt: (b, 0, 0)),
                pl.BlockSpec((1, n_pages, d), lambda b, p, pt: (b, 0, 0)),
                pl.BlockSpec((rows, d), lambda b, p, pt: (b, q_col_block)),
                pl.BlockSpec((rows, d), lambda b, p, pt: (b, 0)),
                pl.BlockSpec((rows, d), lambda b, p, pt: (b, 0)),
                pl.BlockSpec((1, page, d), lambda b, p, pt: (pt[b * n_pages + p], 0, 0)),
            ],
            out_specs=pl.BlockSpec((rows, d), lambda b, p, pt: (b, 0)),
            scratch_shapes=[pltpu.VMEM((n_pages, hr, page), BF16), pltpu.VMEM((hr, d), F32),
                            pltpu.VMEM((hr, 1), F32)],
        ),
        out_shape=jax.ShapeDtypeStruct((n, d), F32),
        compiler_params=_cparams(("parallel", "arbitrary")),
        name="moba_sample_v",
    )(pt, lg, ksum, proj, k_new, v_new, cache_v)


def _merge_kernel(a_ref, b_ref, ga_ref, gb_ref, x_ref, w_ref, gpost_ref, gpre_ref, x1_ref, h2_ref):
    merged = _sigmoid(ga_ref[...]) * a_ref[...] + _sigmoid(gb_ref[...]) * b_ref[...]
    m = _dot(merged.astype(BF16), w_ref[...])
    x1 = x_ref[...] + _rms(m, gpost_ref[...])
    x1_ref[...] = x1
    h2_ref[...] = _rms(x1, gpre_ref[...]).astype(BF16)


def _merge(branch_a, branch_b, proj, x, w_out, g_post, g_pre, tm, ga_block, gb_block):
    n, d = x.shape
    row = lambda cb: pl.BlockSpec((tm, d), lambda i: (i, cb))
    vec = pl.BlockSpec((1, d), lambda i: (0, 0))
    return pl.pallas_call(
        _merge_kernel,
        grid=(n // tm,),
        in_specs=[row(0), row(0), row(ga_block), row(gb_block), row(0),
                  pl.BlockSpec((d, d), lambda i: (0, 0)), vec, vec],
        out_specs=[row(0), row(0)],
        out_shape=[jax.ShapeDtypeStruct((n, d), F32), jax.ShapeDtypeStruct((n, d), BF16)],
        compiler_params=_cparams(("parallel",)),
        name="merge",
    )(branch_a, branch_b, proj, proj, x, w_out, g_post, g_pre)


def _gelu_tanh(x):
    return 0.5 * x * (1.0 + jnp.tanh(math.sqrt(2.0 / math.pi) * (x + 0.044715 * (x * x * x))))


def _causal_conv3(up, prev2, prev1, t, cw, cb):
    p1 = jnp.where(t == 0, prev1, pltpu.roll(up, 1, 0))
    p2 = jnp.where(t == 0, prev2, jnp.where(t == 1, prev1, pltpu.roll(up, 2, 0)))
    return cb + cw[0:1, :] * p2 + cw[1:2, :] * p1 + cw[2:3, :] * up


def _ffn_prompt_kernel(h_ref, x_ref, wup_ref, cw_ref, cb_ref, wdn_ref, g_ref, y_ref, st_ref, carry_scr,
                       *, d_ff, cs):
    tm = h_ref.shape[0]
    h = h_ref[...]
    t = lax.broadcasted_iota(jnp.int32, (tm, cs), 0)

    @pl.when(pl.program_id(1) == 0)
    def _():
        carry_scr[...] = jnp.zeros_like(carry_scr)

    def conv(col0):
        cols = slice(col0, col0 + cs)
        up = _dot(h, wup_ref[:, cols])
        prev1 = carry_scr[7:8, cols]
        prev2 = carry_scr[6:7, cols]
        carry_scr[:, cols] = up[tm - 8:, :]
        st_ref[0, :, cols] = up[tm - (CONV_W - 1):, :]
        return _causal_conv3(up, prev2, prev1, t, cw_ref[:, cols], cb_ref[:, cols])

    acc = jnp.zeros((tm, wdn_ref.shape[1]), F32)
    for c in range(d_ff // cs):
        gate = conv(c * cs)
        lin = conv(d_ff + c * cs)
        acc = acc + _dot((_gelu_tanh(gate) * lin).astype(BF16), wdn_ref[c * cs:(c + 1) * cs, :])
    y_ref[...] = x_ref[...] + _rms(acc, g_ref[...])


def _ffn_sample_kernel(h_ref, x_ref, wg_ref, wl_ref, cwg_ref, cwl_ref, cbg_ref, cbl_ref, wdn_ref, g_ref,
                       s0g_ref, s0l_ref, s1g_ref, s1l_ref, y_ref, upg_ref, upl_ref, acc_scr, *, seq_rows):
    c = pl.program_id(0)
    h = h_ref[...]
    t = lax.broadcasted_iota(jnp.int32, upg_ref.shape, 0) % seq_rows

    @pl.when(c == 0)
    def _():
        acc_scr[...] = jnp.zeros_like(acc_scr)

    upg = _dot(h, wg_ref[...])
    upl = _dot(h, wl_ref[...])
    upg_ref[...] = upg
    upl_ref[...] = upl
    gate = _causal_conv3(upg, s0g_ref[...], s1g_ref[...], t, cwg_ref[...], cbg_ref[...])
    lin = _causal_conv3(upl, s0l_ref[...], s1l_ref[...], t, cwl_ref[...], cbl_ref[...])
    acc_scr[...] += _dot((_gelu_tanh(gate) * lin).astype(BF16), wdn_ref[...])

    @pl.when(c == pl.num_programs(0) - 1)
    def _():
        y_ref[...] = x_ref[...] + _rms(acc_scr[...], g_ref[...])


def _ffn_cs(d_ff):
    for cs in (512, 256, 128):
        if d_ff % cs == 0:
            return cs
    raise ValueError(f"d_ff={d_ff} is not a multiple of the 128-lane tile")


def _ffn_prompt(h2, x1, w_up, conv_w, conv_b, w_down, gain, batch, seq, tm):
    n, d = x1.shape
    d_ff = w_down.shape[0]
    n_t = seq // tm
    full = lambda a: pl.BlockSpec(a.shape, lambda b, t: (0, 0))
    row = pl.BlockSpec((tm, d), lambda b, t: (b * n_t + t, 0))
    return pl.pallas_call(
        functools.partial(_ffn_prompt_kernel, d_ff=d_ff, cs=_ffn_cs(d_ff)),
        grid=(batch, n_t),
        in_specs=[row, row, full(w_up), full(conv_w), full(conv_b), full(w_down), full(gain)],
        out_specs=[row, pl.BlockSpec((1, CONV_W - 1, 2 * d_ff), lambda b, t: (b, 0, 0))],
        out_shape=[jax.ShapeDtypeStruct((n, d), F32),
                   jax.ShapeDtypeStruct((batch, CONV_W - 1, 2 * d_ff), F32)],
        scratch_shapes=[pltpu.VMEM((8, 2 * d_ff), F32)],
        compiler_params=_cparams(("parallel", "arbitrary")),
        name="ffn_prompt",
    )(h2, x1, w_up, conv_w, conv_b, w_down, gain)


def _ffn_sample(h2, x1, w_up, conv_w, conv_b, w_down, gain, state, seq):
    n, d = x1.shape
    d_ff = w_down.shape[0]
    batch = n // seq
    s0 = jnp.broadcast_to(state[:, 0:1, :], (batch, seq, 2 * d_ff)).reshape(n, 2 * d_ff)
    s1 = jnp.broadcast_to(state[:, 1:2, :], (batch, seq, 2 * d_ff)).reshape(n, 2 * d_ff)
    cs = _ffn_cs(d_ff)
    nc = d_ff // cs
    full = lambda a: pl.BlockSpec(a.shape, lambda c: (0, 0))
    gate = lambda rows: pl.BlockSpec((rows, cs), lambda c: (0, c))
    lin = lambda rows: pl.BlockSpec((rows, cs), lambda c: (0, nc + c))
    y, upg, upl = pl.pallas_call(
        functools.partial(_ffn_sample_kernel, seq_rows=seq),
        grid=(nc,),
        in_specs=[full(h2), full(x1), gate(d), lin(d), gate(CONV_W), lin(CONV_W), gate(1), lin(1),
                  pl.BlockSpec((cs, d), lambda c: (c, 0)), full(gain), gate(n), lin(n), gate(n), lin(n)],
        out_specs=[full(x1), gate(n), gate(n)],
        out_shape=[jax.ShapeDtypeStruct((n, d), F32), jax.ShapeDtypeStruct((n, d_ff), F32),
                   jax.ShapeDtypeStruct((n, d_ff), F32)],
        scratch_shapes=[pltpu.VMEM((n, d), F32)],
        compiler_params=_cparams(("arbitrary",)),
        name="ffn_sample",
    )(h2, x1, w_up, w_up, conv_w, conv_w, conv_b, conv_b, w_down, gain, s0, s0, s1, s1)
    up = jnp.concatenate([upg, upl], axis=1)
    return y, up.reshape(batch, seq, 2 * d_ff)[:, seq - (CONV_W - 1):, :]


def _row_tile(n, cap):
    t = min(n, cap)
    while n % t:
        t //= 2
    return t


def kernel(x_prompt, x_sample, cache_k, cache_v, page_table, state_hgrn, state_ffn_conv, norm_mix_pre, norm_mix_post, w_in, hgrn_lb_logits, hgrn_norm, w_out, norm_ffn_pre, norm_ffn_post, w_up, conv_w, conv_b, w_down):
    bp, tp, d = x_prompt.shape
    bs, ts, _ = x_sample.shape
    depth = w_in.shape[0]
    heads = d // HEAD_DIM
    n_pool, page = cache_k.shape[1], cache_k.shape[2]
    lb_all = jnp.cumsum(jax.nn.softmax(hgrn_lb_logits.astype(F32), axis=0), axis=0)

    xp = x_prompt.reshape(bp * tp, d)
    xs = x_sample.reshape(bs * ts, d)
    outs = [[] for _ in range(8)]
    for l in range(depth):
        wl = w_in[l]
        w_perm = jnp.concatenate([wl[:, :5 * d], wl[:, 7 * d:], wl[:, 5 * d:7 * d]], axis=1).astype(BF16)
        w_out_b = w_out[l].astype(BF16)
        w_up_b = w_up[l].astype(BF16)
        w_down_b = w_down[l].astype(BF16)
        vec = lambda a: a[l].reshape(1, -1)
        lb = lb_all[l].reshape(1, d)
        Q_BLK, GA_BLK, GB_BLK = 4, 5, 6

        def mixers(x, batch, seq, s0, moba):
            proj, k_new, v_new = _inproj(x, vec(norm_mix_pre), w_perm, _row_tile(x.shape[0], 1024))
            chunk = math.gcd(seq, HGRN_CHUNK)
            t_tile = _row_tile(seq, 256)
            branch_a, s_new = _hgrn(proj, lb, vec(hgrn_norm), s0, batch, seq, t_tile, chunk)
            branch_b = moba(proj, k_new, v_new)
            x1, h2 = _merge(branch_a, branch_b, proj, x, w_out_b, vec(norm_mix_post), vec(norm_ffn_pre),
                            _row_tile(x.shape[0], 512), GA_BLK, GB_BLK)
            return x1, h2, k_new, v_new, s_new

        s0p = jnp.zeros((bp, heads, HEAD_DIM, HEAD_DIM), F32)
        x1, h2, kp, vp, sp = mixers(xp, bp, tp, s0p,
                                    lambda pr, k, v: _moba_prompt(pr, k, v, bp, tp, Q_BLK))
        xp, cp = _ffn_prompt(h2, x1, w_up_b, conv_w[l], conv_b[l].reshape(1, -1), w_down_b,
                             vec(norm_ffn_post), bp, tp, _row_tile(tp, 512))
        ck = cache_k[l].reshape(n_pool, page, d)
        cv = cache_v[l].reshape(n_pool, page, d)
        x1, h2, kn, vn, sn = mixers(xs, bs, ts, state_hgrn[l],
                                    lambda pr, k, v: _moba_sample(pr, k, v, ck, cv, page_table, ts, Q_BLK))
        xs, cn = _ffn_sample(h2, x1, w_up_b, conv_w[l], conv_b[l].reshape(1, -1), w_down_b,
                             vec(norm_ffn_post), state_ffn_conv[l], ts)
        kv = lambda a, b, t: a.reshape(b, t, heads, HEAD_DIM)
        for o, val in zip(outs, (kv(kp, bp, tp), kv(vp, bp, tp), sp, cp, kv(kn, bs, ts), kv(vn, bs, ts), sn, cn)):
            o.append(val)
    return (xp.reshape(bp, tp, d), xs.reshape(bs, ts, d), *(jnp.stack(o) for o in outs))
```

```python
import functools
import math

import jax
import jax.numpy as jnp
from jax import lax
from jax.experimental import pallas as pl
from jax.experimental.pallas import tpu as pltpu

F32 = jnp.float32
BF16 = jnp.bfloat16

NORM_EPS = 1e-6
HEAD_DIM = 128
MOBA_BLOCK = 256
MOBA_TOPK = 3
HGRN_CHUNK = 64
CONV_W = 3
BF16_ROWS = 16
VMEM_LIMIT = 56 * 1024 * 1024


def _cparams(sem):
    return pltpu.CompilerParams(dimension_semantics=sem, vmem_limit_bytes=VMEM_LIMIT)


def _sigmoid(x):
    return 1.0 / (1.0 + jnp.exp(-x))


def _rms(x, gain):
    return x * lax.rsqrt(jnp.mean(x * x, axis=-1, keepdims=True) + NORM_EPS) * gain


def _dot(a, b):
    return jnp.dot(a, b, preferred_element_type=F32)


def _dot_nt(a, b):
    return lax.dot_general(a, b, (((1,), (1,)), ((), ())), preferred_element_type=F32)


def _dot_tn(a, b):
    return lax.dot_general(a, b, (((0,), (0,)), ((), ())), preferred_element_type=F32)


def _split_bf16(x, parts):
    out = []
    for _ in range(parts):
        p = x.astype(BF16)
        out.append(p)
        x = x - p.astype(F32)
    return out


def _dot_nt_split(a, b, operand_dtype=BF16):
    ah, al = (p.astype(operand_dtype) for p in _split_bf16(a, 2))
    bh, bl = (p.astype(operand_dtype) for p in _split_bf16(b, 2))
    return _dot_nt(ah, bh) + _dot_nt(ah, bl) + _dot_nt(al, bh)


def _inproj_kernel(x_ref, g_ref, w_ref, p_ref, k_ref, v_ref, h_scr, *, n_main):
    j = pl.program_id(1)

    @pl.when(j == 0)
    def _():
        h_scr[...] = _rms(x_ref[...], g_ref[...]).astype(BF16)

    @pl.when(j < n_main)
    def _():
        p_ref[...] = _dot(h_scr[...], w_ref[...])

    @pl.when(j == n_main)
    def _():
        k_ref[...] = _dot(h_scr[...], w_ref[...])

    @pl.when(j == n_main + 1)
    def _():
        v_ref[...] = _dot(h_scr[...], w_ref[...])


def _inproj(x, gain, w_perm, tm):
    n, d = x.shape
    n_blocks = w_perm.shape[1] // d
    n_main = n_blocks - 2
    return pl.pallas_call(
        functools.partial(_inproj_kernel, n_main=n_main),
        grid=(n // tm, n_blocks),
        in_specs=[
            pl.BlockSpec((tm, d), lambda i, j: (i, 0)),
            pl.BlockSpec((1, d), lambda i, j: (0, 0)),
            pl.BlockSpec((d, d), lambda i, j: (0, j)),
        ],
        out_specs=[
            pl.BlockSpec((tm, d), lambda i, j: (i, jnp.minimum(j, n_main - 1))),
            pl.BlockSpec((tm, d), lambda i, j: (i, 0)),
            pl.BlockSpec((tm, d), lambda i, j: (i, 0)),
        ],
        out_shape=[
            jax.ShapeDtypeStruct((n, n_main * d), F32),
            jax.ShapeDtypeStruct((n, d), F32),
            jax.ShapeDtypeStruct((n, d), F32),
        ],
        scratch_shapes=[pltpu.VMEM((tm, d), BF16)],
        compiler_params=_cparams(("parallel", "arbitrary")),
        name="inproj",
    )(x, gain, w_perm)


def _hgrn_kernel(hq_ref, hf_ref, hi_ref, hg_ref, lb_ref, gain_ref, s0_ref, a_ref, sout_ref, st_scr,
                 *, chunk, n_chunks, heads):
    tt = pl.program_id(1)

    @pl.when(tt == 0)
    def _():
        for h in range(heads):
            st_scr[h] = s0_ref[0, h].T

    row = lax.broadcasted_iota(jnp.int32, (chunk, chunk), 0)
    col = lax.broadcasted_iota(jnp.int32, (chunk, chunk), 1)
    causal = row >= col
    mm_dtype = BF16 if chunk % 16 == 0 else F32
    mm = lambda a: a.astype(mm_dtype)
    tril = mm(jnp.where(causal, 1.0, 0.0))

    for h in range(heads):
        sl = slice(h * HEAD_DIM, (h + 1) * HEAD_DIM)
        lb = lb_ref[:, sl]
        gain = gain_ref[:, sl]
        st = st_scr[h]
        for c in range(n_chunks):
            rs = slice(c * chunk, (c + 1) * chunk)
            hq = hq_ref[rs, sl]
            q = hq * _sigmoid(hq)
            f = lb + (1.0 - lb) * _sigmoid(hf_ref[rs, sl])
            kk = 1.0 - f
            b = sum(_dot(tril, mm(part)) for part in _split_bf16(jnp.log(f), 3))
            b_last = b[chunk - 1:chunk, :]
            qt = mm(q * jnp.exp(b))
            kt = mm(kk * jnp.exp(-b))
            kh = mm(kk * jnp.exp(b_last - b))
            vb = mm(hi_ref[rs, sl])
            scores = jnp.where(causal, _dot_nt(qt, kt), 0.0)
            o = _dot(mm(scores), vb) + _dot_nt(qt, mm(st))
            st = st * jnp.exp(b_last) + _dot_tn(vb, kh)
            hg = hg_ref[rs, sl]
            a_ref[rs, sl] = _rms(o, gain) * (hg * _sigmoid(hg))
        st_scr[h] = st

    @pl.when(tt == pl.num_programs(1) - 1)
    def _():
        for h in range(heads):
            sout_ref[0, h] = st_scr[h].T


def _hgrn(proj, lb, gain, s0, batch, seq, t_tile, chunk):
    n = proj.shape[0]
    d = lb.shape[1]
    heads = d // HEAD_DIM
    n_t = seq // t_tile
    col_spec = lambda cb: pl.BlockSpec((t_tile, d), lambda b, t: (b * n_t + t, cb))
    state_spec = pl.BlockSpec((1, heads, HEAD_DIM, HEAD_DIM), lambda b, t: (b, 0, 0, 0))
    return pl.pallas_call(
        functools.partial(_hgrn_kernel, chunk=chunk, n_chunks=t_tile // chunk, heads=heads),
        grid=(batch, n_t),
        in_specs=[col_spec(0), col_spec(1), col_spec(2), col_spec(3),
                  pl.BlockSpec((1, d), lambda b, t: (0, 0)),
                  pl.BlockSpec((1, d), lambda b, t: (0, 0)),
                  state_spec],
        out_specs=[pl.BlockSpec((t_tile, d), lambda b, t: (b * n_t + t, 0)), state_spec],
        out_shape=[jax.ShapeDtypeStruct((n, d), F32),
                   jax.ShapeDtypeStruct((batch, heads, HEAD_DIM, HEAD_DIM), F32)],
        scratch_shapes=[pltpu.VMEM((heads, HEAD_DIM, HEAD_DIM), F32)],
        compiler_params=_cparams(("parallel", "arbitrary")),
        name="hgrn",
    )(proj, proj, proj, proj, lb, gain, s0)


def _topk_bias(s, valid, topk):
    n_blocks = len(valid)
    rows = []
    for n in range(n_blocks):
        sn = s[n:n + 1, :]
        cnt = jnp.zeros(sn.shape, jnp.int32)
        for m in range(n_blocks):
            if m == n:
                continue
            sm = s[m:m + 1, :]
            beats = (sm >= sn) if m < n else (sm > sn)
            cnt = cnt + jnp.where(beats & valid[m], 1, 0)
        rows.append(jnp.where(cnt < topk, 0.0, -jnp.inf))
    return rows


def _moba_prompt_kernel(q_ref, k_ref, v_ref, o_ref, *, nb, blk):
    t = nb * blk
    scale = 1.0 / math.sqrt(HEAD_DIM)
    q = q_ref[...]
    k = k_ref[...]
    qb = q.astype(BF16)
    kb = k.astype(BF16)
    vt = v_ref[...].T.astype(BF16)

    kmean = jnp.concatenate(
        [jnp.mean(k[n * blk:(n + 1) * blk, :], axis=0, keepdims=True) for n in range(nb)]
        + [jnp.zeros((BF16_ROWS - nb % BF16_ROWS, HEAD_DIM), F32)], axis=0)
    s = _dot_nt_split(kmean, q)
    q_blk = lax.broadcasted_iota(jnp.int32, (1, t), 1) // blk
    bias = _topk_bias(s, [q_blk > m for m in range(nb - 1)], MOBA_TOPK)

    key = lax.broadcasted_iota(jnp.int32, (blk, blk), 0)
    qry = lax.broadcasted_iota(jnp.int32, (blk, blk), 1)
    for j in range(nb):
        qs = slice(j * blk, (j + 1) * blk)
        lg = _dot_nt(kb[:(j + 1) * blk, :], qb[qs, :]) * scale
        parts = [lg[n * blk:(n + 1) * blk, :] + bias[n][:, qs] for n in range(j)]
        parts.append(jnp.where(key <= qry, lg[j * blk:, :], -jnp.inf))
        m = functools.reduce(jnp.maximum, [jnp.max(p, axis=0, keepdims=True) for p in parts])
        ps = [jnp.exp(p - m) for p in parts]
        l = functools.reduce(jnp.add, [jnp.sum(p, axis=0, keepdims=True) for p in ps])
        pb = jnp.concatenate([p.astype(BF16) for p in ps], axis=0)
        o_ref[qs, :] = (_dot(vt[:, :(j + 1) * blk], pb) / l).T


def _moba_prompt(proj, k_new, v_new, batch, seq, q_col_block):
    n, d = k_new.shape
    heads = d // HEAD_DIM
    blk = MOBA_BLOCK
    nb = seq // blk
    seq_head = lambda cb: pl.BlockSpec((seq, HEAD_DIM), lambda b, h: (b, cb * heads + h))
    return pl.pallas_call(
        functools.partial(_moba_prompt_kernel, nb=nb, blk=blk),
        grid=(batch, heads),
        in_specs=[seq_head(q_col_block), seq_head(0), seq_head(0)],
        out_specs=seq_head(0),
        out_shape=jax.ShapeDtypeStruct((n, d), F32),
        compiler_params=_cparams(("parallel", "parallel")),
        name="moba_prompt",
    )(proj, k_new, v_new)


def _block_diag_queries(q, heads, rows):
    d = q.shape[1]
    tiled = jnp.concatenate([q] * heads, axis=0)
    r = lax.broadcasted_iota(jnp.int32, (heads * rows, d), 0) // rows
    c = lax.broadcasted_iota(jnp.int32, (heads * rows, d), 1) // HEAD_DIM
    return jnp.where(r == c, tiled, 0.0)


def _moba_sample_ksum_kernel(pt_ref, *refs):
    del pt_ref
    *k_refs, ksum_ref = refs
    g = pl.program_id(1)
    for i, k_ref in enumerate(k_refs):
        ksum_ref[0, g * len(k_refs) + i] = jnp.sum(k_ref[...], axis=0)


def _moba_sample_attn_kernel(pt_ref, ksum_ref, q_ref, kn_ref, vn_ref, *refs, heads, rows, page, blocks_per_step):
    del pt_ref
    n_in = 2 * blocks_per_step
    k_refs, v_refs = refs[:n_in], refs[n_in:2 * n_in]
    o_ref, kmean_scr, bias_scr, hmask_scr, m_scr, l_scr, acc_scr = refs[2 * n_in:]
    g = pl.program_id(1)
    hr = heads * rows
    d = heads * HEAD_DIM
    nb = bias_scr.shape[0]
    scale = 1.0 / math.sqrt(HEAD_DIM)
    head = lambda h: slice(h * HEAD_DIM, (h + 1) * HEAD_DIM)

    q = q_ref[...]

    @pl.when(g == 0)
    def _():
        for n in range(nb):
            kmean_scr[n * heads:(n + 1) * heads, :] = (ksum_ref[0, 2 * n] + ksum_ref[0, 2 * n + 1]) * (0.5 / page)
        s = jnp.concatenate([_dot_nt_split(q[:, head(h)], kmean_scr[pl.ds(h, nb, stride=heads), :], F32)
                             for h in range(heads)], axis=0)
        cnt = jnp.zeros((hr, nb), jnp.int32)
        idx = lax.broadcasted_iota(jnp.int32, (hr, nb), 1)
        for m in range(nb):
            sm = s[:, m:m + 1]
            beats = (sm > s) | ((sm == s) & (idx > m))
            cnt = cnt + jnp.where(beats, 1, 0)
        bias = jnp.where(cnt < MOBA_TOPK, 0.0, -jnp.inf)
        for n in range(nb):
            col = jnp.broadcast_to(bias[:, n:n + 1], (hr, HEAD_DIM))
            bias_scr[n] = jnp.concatenate([col, col], axis=0)
        r = lax.broadcasted_iota(jnp.int32, hmask_scr.shape, 0)
        c = lax.broadcasted_iota(jnp.int32, hmask_scr.shape, 1)
        hmask_scr[...] = jnp.where(c % heads == (r % hr) // rows, 0.0, -jnp.inf)

        qbd = _block_diag_queries(q, heads, rows).astype(BF16)
        kn = jnp.concatenate([kn_ref[...], jnp.zeros((page - rows, d), F32)], axis=0)
        vn = jnp.concatenate([vn_ref[...], jnp.zeros((page - rows, d), F32)], axis=0)
        r_own = lax.broadcasted_iota(jnp.int32, (hr, page), 0) % rows
        c_own = lax.broadcasted_iota(jnp.int32, (hr, page), 1)
        lg_own = jnp.where(c_own <= r_own, _dot_nt(qbd, kn.astype(BF16)) * scale, -jnp.inf)
        m0 = jnp.max(lg_own, axis=1, keepdims=True)
        p_own = jnp.exp(lg_own - m0)
        pv = _dot(p_own.astype(BF16), vn.astype(BF16))
        m_scr[...] = m0
        l_scr[...] = jnp.sum(p_own, axis=1, keepdims=True)
        acc_scr[...] = jnp.concatenate([pv[h * rows:(h + 1) * rows, head(h)] for h in range(heads)], axis=0)

    q_hd = jnp.concatenate([q[:, head(h)] for h in range(heads)], axis=0).astype(BF16)
    zero = jnp.zeros_like(q_hd)
    q_pair = jnp.concatenate([jnp.concatenate([q_hd, zero], axis=1),
                              jnp.concatenate([zero, q_hd], axis=1)], axis=0)
    pos_head = lambda ref: ref[...].reshape(page * heads, HEAD_DIM).astype(BF16)
    for i in range(blocks_per_step):
        kk = jnp.concatenate([pos_head(k_refs[2 * i]), pos_head(k_refs[2 * i + 1])], axis=1)
        vv = jnp.concatenate([pos_head(v_refs[2 * i]), pos_head(v_refs[2 * i + 1])], axis=1)
        bias = bias_scr[g * blocks_per_step + i]
        lg = _dot_nt(q_pair, kk) * scale + hmask_scr[...] + jnp.concatenate([bias] * heads, axis=1)
        m_row = jnp.max(lg, axis=1, keepdims=True)
        m_old = m_scr[...]
        m_new = jnp.maximum(m_old, jnp.maximum(m_row[:hr], m_row[hr:]))
        alpha = jnp.exp(m_old - m_new)
        p = jnp.exp(lg - jnp.concatenate([m_new, m_new], axis=0))
        l_row = jnp.sum(p, axis=1, keepdims=True)
        pv = _dot(p.astype(BF16), vv)
        m_scr[...] = m_new
        l_scr[...] = alpha * l_scr[...] + l_row[:hr] + l_row[hr:]
        acc_scr[...] = alpha * acc_scr[...] + pv[:hr, :HEAD_DIM] + pv[hr:, HEAD_DIM:]

    @pl.when(g == pl.num_programs(1) - 1)
    def _():
        out = acc_scr[...] / l_scr[...]
        for h in range(heads):
            o_ref[:, head(h)] = out[h * rows:(h + 1) * rows, :]


def _moba_sample(proj, k_new, v_new, cache_k, cache_v, page_table, layer, rows, q_col_block):
    n, d = k_new.shape
    _, _, page, heads, dh = cache_k.shape
    batch, n_pages = page_table.shape
    assert dh == HEAD_DIM and 2 * page == MOBA_BLOCK and n_pages % 2 == 0
    nb = n_pages // 2
    hr = heads * rows
    pt = page_table.reshape(-1)

    def page_spec(per_step, i):
        return pl.BlockSpec((None, None, page, heads, dh),
                            lambda b, g, pt: (layer, pt[b * n_pages + g * per_step + i], 0, 0, 0))

    sum_pages = math.gcd(n_pages, 8)
    ksum_spec = pl.BlockSpec((1, n_pages, heads, dh), lambda b, g, pt: (b, 0, 0, 0))
    ksum = pl.pallas_call(
        _moba_sample_ksum_kernel,
        grid_spec=pltpu.PrefetchScalarGridSpec(
            num_scalar_prefetch=1,
            grid=(batch, n_pages // sum_pages),
            in_specs=[page_spec(sum_pages, i) for i in range(sum_pages)],
            out_specs=ksum_spec,
        ),
        out_shape=jax.ShapeDtypeStruct((batch, n_pages, heads, dh), F32),
        compiler_params=_cparams(("parallel", "arbitrary")),
        name="moba_sample_ksum",
    )(pt, *([cache_k] * sum_pages))

    bps = math.gcd(nb, 2)
    row_spec = lambda cb: pl.BlockSpec((rows, d), lambda b, g, pt: (b, cb))
    pages = [page_spec(2 * bps, i) for i in range(2 * bps)]
    return pl.pallas_call(
        functools.partial(_moba_sample_attn_kernel, heads=heads, rows=rows, page=page, blocks_per_step=bps),
        grid_spec=pltpu.PrefetchScalarGridSpec(
            num_scalar_prefetch=1,
            grid=(batch, nb // bps),
            in_specs=[ksum_spec, row_spec(q_col_block), row_spec(0), row_spec(0)] + pages + pages,
            out_specs=row_spec(0),
            scratch_shapes=[pltpu.VMEM((nb * heads, dh), F32),
                            pltpu.VMEM((nb, 2 * hr, dh), F32),
                            pltpu.VMEM((2 * hr, page * heads), F32),
                            pltpu.VMEM((hr, 1), F32), pltpu.VMEM((hr, 1), F32), pltpu.VMEM((hr, dh), F32)],
        ),
        out_shape=jax.ShapeDtypeStruct((n, d), F32),
        compiler_params=_cparams(("parallel", "arbitrary")),
        name="moba_sample_attn",
    )(pt, ksum, proj, k_new, v_new, *([cache_k] * (2 * bps)), *([cache_v] * (2 * bps)))


def _merge_kernel(a_ref, b_ref, ga_ref, gb_ref, x_ref, w_ref, gpost_ref, gpre_ref, x1_ref, h2_ref):
    merged = _sigmoid(ga_ref[...]) * a_ref[...] + _sigmoid(gb_ref[...]) * b_ref[...]
    m = _dot(merged.astype(BF16), w_ref[...])
    x1 = x_ref[...] + _rms(m, gpost_ref[...])
    x1_ref[...] = x1
    h2_ref[...] = _rms(x1, gpre_ref[...]).astype(BF16)


def _merge(branch_a, branch_b, proj, x, w_out, g_post, g_pre, tm, ga_block, gb_block):
    n, d = x.shape
    row = lambda cb: pl.BlockSpec((tm, d), lambda i: (i, cb))
    vec = pl.BlockSpec((1, d), lambda i: (0, 0))
    return pl.pallas_call(
        _merge_kernel,
        grid=(n // tm,),
        in_specs=[row(0), row(0), row(ga_block), row(gb_block), row(0),
                  pl.BlockSpec((d, d), lambda i: (0, 0)), vec, vec],
        out_specs=[row(0), row(0)],
        out_shape=[jax.ShapeDtypeStruct((n, d), F32), jax.ShapeDtypeStruct((n, d), BF16)],
        compiler_params=_cparams(("parallel",)),
        name="merge",
    )(branch_a, branch_b, proj, proj, x, w_out, g_post, g_pre)


def _gelu_tanh(x):
    return 0.5 * x * (1.0 + jnp.tanh(math.sqrt(2.0 / math.pi) * (x + 0.044715 * (x * x * x))))


def _causal_conv3(up, prev2, prev1, t, cw, cb):
    p1 = jnp.where(t == 0, prev1, pltpu.roll(up, 1, 0))
    p2 = jnp.where(t == 0, prev2, jnp.where(t == 1, prev1, pltpu.roll(up, 2, 0)))
    return cb + cw[0:1, :] * p2 + cw[1:2, :] * p1 + cw[2:3, :] * up


def _ffn_prompt_kernel(h_ref, x_ref, wup_ref, cw_ref, cb_ref, wdn_ref, g_ref, y_ref, st_ref,
                       carry_scr, win_scr, act_scr, *, d_ff, cs):
    tm = h_ref.shape[0]
    h = h_ref[...]
    halo = carry_scr.shape[0]

    @pl.when(pl.program_id(1) == 0)
    def _():
        carry_scr[...] = jnp.zeros_like(carry_scr)

    def conv(col0, win):
        cols = slice(col0, col0 + cs)
        up = _dot(h, wup_ref[:, cols])
        win[0:halo, :] = carry_scr[:, cols]
        win[halo:, :] = up
        carry_scr[:, cols] = up[tm - halo:, :]
        st_ref[0, :, cols] = up[tm - (CONV_W - 1):, :]
        cw = cw_ref[:, cols]
        return (cb_ref[:, cols] + cw[0:1, :] * win[halo - 2:halo - 2 + tm, :]
                + cw[1:2, :] * win[halo - 1:halo - 1 + tm, :] + cw[2:3, :] * up)

    for c in range(d_ff // cs):
        gate = conv(c * cs, win_scr.at[(2 * c) % win_scr.shape[0]])
        lin = conv(d_ff + c * cs, win_scr.at[(2 * c + 1) % win_scr.shape[0]])
        act_scr[:, c * cs:(c + 1) * cs] = (_gelu_tanh(gate) * lin).astype(BF16)
    y_ref[...] = x_ref[...] + _rms(_dot(act_scr[...], wdn_ref[...]), g_ref[...])


def _ffn_sample_kernel(h_ref, x_ref, wg_ref, wl_ref, cwg_ref, cwl_ref, cbg_ref, cbl_ref, wdn_ref, g_ref,
                       s0g_ref, s0l_ref, s1g_ref, s1l_ref, y_ref, upg_ref, upl_ref, acc_scr, *, seq_rows):
    c = pl.program_id(0)
    h = h_ref[...]
    t = lax.broadcasted_iota(jnp.int32, upg_ref.shape, 0) % seq_rows

    @pl.when(c == 0)
    def _():
        acc_scr[...] = jnp.zeros_like(acc_scr)

    upg = _dot(h, wg_ref[...])
    upl = _dot(h, wl_ref[...])
    upg_ref[...] = upg
    upl_ref[...] = upl
    gate = _causal_conv3(upg, s0g_ref[...], s1g_ref[...], t, cwg_ref[...], cbg_ref[...])
    lin = _causal_conv3(upl, s0l_ref[...], s1l_ref[...], t, cwl_ref[...], cbl_ref[...])
    acc_scr[...] += _dot((_gelu_tanh(gate) * lin).astype(BF16), wdn_ref[...])

    @pl.when(c == pl.num_programs(0) - 1)
    def _():
        y_ref[...] = x_ref[...] + _rms(acc_scr[...], g_ref[...])


def _ffn_cs(d_ff):
    for cs in (512, 256, 128):
        if d_ff % cs == 0:
            return cs
    raise ValueError(f"d_ff={d_ff} is not a multiple of the 128-lane tile")


def _ffn_prompt(h2, x1, w_up, conv_w, conv_b, w_down, gain, batch, seq, tm):
    n, d = x1.shape
    d_ff = w_down.shape[0]
    n_t = seq // tm
    full = lambda a: pl.BlockSpec(a.shape, lambda b, t: (0, 0))
    row = pl.BlockSpec((tm, d), lambda b, t: (b * n_t + t, 0))
    cs = _ffn_cs(d_ff)
    halo = 8
    return pl.pallas_call(
        functools.partial(_ffn_prompt_kernel, d_ff=d_ff, cs=cs),
        grid=(batch, n_t),
        in_specs=[row, row, full(w_up), full(conv_w), full(conv_b), full(w_down), full(gain)],
        out_specs=[row, pl.BlockSpec((1, CONV_W - 1, 2 * d_ff), lambda b, t: (b, 0, 0))],
        out_shape=[jax.ShapeDtypeStruct((n, d), F32),
                   jax.ShapeDtypeStruct((batch, CONV_W - 1, 2 * d_ff), F32)],
        scratch_shapes=[pltpu.VMEM((halo, 2 * d_ff), F32), pltpu.VMEM((4, halo + tm, cs), F32),
                        pltpu.VMEM((tm, d_ff), BF16)],
        compiler_params=_cparams(("parallel", "arbitrary")),
        name="ffn_prompt",
    )(h2, x1, w_up, conv_w, conv_b, w_down, gain)


def _ffn_sample(h2, x1, w_up, conv_w, conv_b, w_down, gain, state, seq):
    n, d = x1.shape
    d_ff = w_down.shape[0]
    batch = n // seq
    s0 = jnp.broadcast_to(state[:, 0:1, :], (batch, seq, 2 * d_ff)).reshape(n, 2 * d_ff)
    s1 = jnp.broadcast_to(state[:, 1:2, :], (batch, seq, 2 * d_ff)).reshape(n, 2 * d_ff)
    cs = _ffn_cs(d_ff)
    nc = d_ff // cs
    full = lambda a: pl.BlockSpec(a.shape, lambda c: (0, 0))
    gate = lambda rows: pl.BlockSpec((rows, cs), lambda c: (0, c))
    lin = lambda rows: pl.BlockSpec((rows, cs), lambda c: (0, nc + c))
    y, upg, upl = pl.pallas_call(
        functools.partial(_ffn_sample_kernel, seq_rows=seq),
        grid=(nc,),
        in_specs=[full(h2), full(x1), gate(d), lin(d), gate(CONV_W), lin(CONV_W), gate(1), lin(1),
                  pl.BlockSpec((cs, d), lambda c: (c, 0)), full(gain), gate(n), lin(n), gate(n), lin(n)],
        out_specs=[full(x1), gate(n), gate(n)],
        out_shape=[jax.ShapeDtypeStruct((n, d), F32), jax.ShapeDtypeStruct((n, d_ff), F32),
                   jax.ShapeDtypeStruct((n, d_ff), F32)],
        scratch_shapes=[pltpu.VMEM((n, d), F32)],
        compiler_params=_cparams(("arbitrary",)),
        name="ffn_sample",
    )(h2, x1, w_up, w_up, conv_w, conv_w, conv_b, conv_b, w_down, gain, s0, s0, s1, s1)
    up = jnp.concatenate([upg, upl], axis=1)
    return y, up.reshape(batch, seq, 2 * d_ff)[:, seq - (CONV_W - 1):, :]


def _row_tile(n, cap):
    t = min(n, cap)
    while n % t:
        t //= 2
    return t


def kernel(x_prompt, x_sample, cache_k, cache_v, page_table, state_hgrn, state_ffn_conv, norm_mix_pre, norm_mix_post, w_in, hgrn_lb_logits, hgrn_norm, w_out, norm_ffn_pre, norm_ffn_post, w_up, conv_w, conv_b, w_down):
    bp, tp, d = x_prompt.shape
    bs, ts, _ = x_sample.shape
    depth = w_in.shape[0]
    heads = d // HEAD_DIM
    n_pool, page = cache_k.shape[1], cache_k.shape[2]
    lb_all = jnp.cumsum(jax.nn.softmax(hgrn_lb_logits.astype(F32), axis=0), axis=0)

    xp = x_prompt.reshape(bp * tp, d)
    xs = x_sample.reshape(bs * ts, d)
    outs = [[] for _ in range(8)]
    for l in range(depth):
        wl = w_in[l]
        w_perm = jnp.concatenate([wl[:, :5 * d], wl[:, 7 * d:], wl[:, 5 * d:7 * d]], axis=1).astype(BF16)
        w_out_b = w_out[l].astype(BF16)
        w_up_b = w_up[l].astype(BF16)
        w_down_b = w_down[l].astype(BF16)
        vec = lambda a: a[l].reshape(1, -1)
        lb = lb_all[l].reshape(1, d)
        Q_BLK, GA_BLK, GB_BLK = 4, 5, 6

        def mixers(x, batch, seq, s0, moba):
            proj, k_new, v_new = _inproj(x, vec(norm_mix_pre), w_perm, _row_tile(x.shape[0], 1024))
            chunk = math.gcd(seq, HGRN_CHUNK)
            t_tile = _row_tile(seq, 256)
            branch_a, s_new = _hgrn(proj, lb, vec(hgrn_norm), s0, batch, seq, t_tile, chunk)
            branch_b = moba(proj, k_new, v_new)
            x1, h2 = _merge(branch_a, branch_b, proj, x, w_out_b, vec(norm_mix_post), vec(norm_ffn_pre),
                            _row_tile(x.shape[0], 512), GA_BLK, GB_BLK)
            return x1, h2, k_new, v_new, s_new

        s0p = jnp.zeros((bp, heads, HEAD_DIM, HEAD_DIM), F32)
        x1, h2, kp, vp, sp = mixers(xp, bp, tp, s0p,
                                    lambda pr, k, v: _moba_prompt(pr, k, v, bp, tp, Q_BLK))
        xp, cp = _ffn_prompt(h2, x1, w_up_b, conv_w[l], conv_b[l].reshape(1, -1), w_down_b,
                             vec(norm_ffn_post), bp, tp, _row_tile(tp, 512))
        x1, h2, kn, vn, sn = mixers(xs, bs, ts, state_hgrn[l],
                                    lambda pr, k, v: _moba_sample(pr, k, v, cache_k, cache_v, page_table, l, ts, Q_BLK))
        xs, cn = _ffn_sample(h2, x1, w_up_b, conv_w[l], conv_b[l].reshape(1, -1), w_down_b,
                             vec(norm_ffn_post), state_ffn_conv[l], ts)
        kv = lambda a, b, t: a.reshape(b, t, heads, HEAD_DIM)
        for o, val in zip(outs, (kv(kp, bp, tp), kv(vp, bp, tp), sp, cp, kv(kn, bs, ts), kv(vn, bs, ts), sn, cn)):
            o.append(val)
    return (xp.reshape(bp, tp, d), xs.reshape(bs, ts, d), *(jnp.stack(o) for o in outs))
```

```python
import functools
import math

import jax
import jax.numpy as jnp
from jax import lax
from jax.experimental import pallas as pl
from jax.experimental.pallas import tpu as pltpu

F32 = jnp.float32
BF16 = jnp.bfloat16

NORM_EPS = 1e-6
HEAD_DIM = 128
MOBA_BLOCK = 256
MOBA_TOPK = 3
HGRN_CHUNK = 64
CONV_W = 3
BF16_ROWS = 16
SOFTMAX_LOG2_SCALE = math.log2(math.e) / math.sqrt(HEAD_DIM)
VMEM_LIMIT = 56 * 1024 * 1024


def _cparams(sem):
    return pltpu.CompilerParams(dimension_semantics=sem, vmem_limit_bytes=VMEM_LIMIT)


def _sigmoid(x):
    return 1.0 / (1.0 + jnp.exp(-x))


def _rms(x, gain):
    return x * lax.rsqrt(jnp.mean(x * x, axis=-1, keepdims=True) + NORM_EPS) * gain


def _dot(a, b):
    return jnp.dot(a, b, preferred_element_type=F32)


def _dot_nt(a, b):
    return lax.dot_general(a, b, (((1,), (1,)), ((), ())), preferred_element_type=F32)


def _dot_tn(a, b):
    return lax.dot_general(a, b, (((0,), (0,)), ((), ())), preferred_element_type=F32)


def _split_bf16(x, parts):
    out = []
    for _ in range(parts):
        p = x.astype(BF16)
        out.append(p)
        x = x - p.astype(F32)
    return out


def _dot_nt_split(a, b, operand_dtype=BF16):
    ah, al = (p.astype(operand_dtype) for p in _split_bf16(a, 2))
    bh, bl = (p.astype(operand_dtype) for p in _split_bf16(b, 2))
    return _dot_nt(ah, bh) + _dot_nt(ah, bl) + _dot_nt(al, bh)


def _inproj_kernel(x_ref, g_ref, w_ref, p_ref, k_ref, v_ref, h_scr, *, n_main):
    j = pl.program_id(1)

    @pl.when(j == 0)
    def _():
        h_scr[...] = _rms(x_ref[...], g_ref[...]).astype(BF16)

    @pl.when(j < n_main)
    def _():
        p_ref[...] = _dot(h_scr[...], w_ref[...])

    @pl.when(j == n_main)
    def _():
        k_ref[...] = _dot(h_scr[...], w_ref[...])

    @pl.when(j == n_main + 1)
    def _():
        v_ref[...] = _dot(h_scr[...], w_ref[...])


def _inproj(x, gain, w_perm, tm):
    n, d = x.shape
    n_blocks = w_perm.shape[1] // d
    n_main = n_blocks - 2
    return pl.pallas_call(
        functools.partial(_inproj_kernel, n_main=n_main),
        grid=(n // tm, n_blocks),
        in_specs=[
            pl.BlockSpec((tm, d), lambda i, j: (i, 0)),
            pl.BlockSpec((1, d), lambda i, j: (0, 0)),
            pl.BlockSpec((d, d), lambda i, j: (0, j)),
        ],
        out_specs=[
            pl.BlockSpec((tm, d), lambda i, j: (i, jnp.minimum(j, n_main - 1))),
            pl.BlockSpec((tm, d), lambda i, j: (i, 0)),
            pl.BlockSpec((tm, d), lambda i, j: (i, 0)),
        ],
        out_shape=[
            jax.ShapeDtypeStruct((n, n_main * d), F32),
            jax.ShapeDtypeStruct((n, d), F32),
            jax.ShapeDtypeStruct((n, d), F32),
        ],
        scratch_shapes=[pltpu.VMEM((tm, d), BF16)],
        compiler_params=_cparams(("parallel", "arbitrary")),
        name="inproj",
    )(x, gain, w_perm)


def _hgrn_kernel(hq_ref, hf_ref, hi_ref, hg_ref, lb_ref, gain_ref, s0_ref, a_ref, sout_ref, st_scr, *, heads):
    tt = pl.program_id(1)
    C = HGRN_CHUNK
    assert 2 * C == HEAD_DIM
    rows = hq_ref.shape[0]
    d = heads * HEAD_DIM
    pad = (-rows) % C
    t = rows + pad

    @pl.when(tt == 0)
    def _():
        for h in range(heads):
            st_scr[h] = s0_ref[0, h].T

    def padded(x, fill):
        return jnp.concatenate([x, jnp.full((pad, d), fill, F32)], axis=0) if pad else x

    lb = lb_ref[...]
    f = padded(lb + (1.0 - lb) * _sigmoid(hf_ref[...]), 1.0)
    kk = 1.0 - f
    hq = hq_ref[...]
    q = padded(hq * _sigmoid(hq), 0.0)
    v = padded(hi_ref[...], 0.0)
    hg = hg_ref[...]
    gate = hg * _sigmoid(hg)
    gain = gain_ref[...]
    tril = jnp.where(lax.broadcasted_iota(jnp.int32, (t, t), 0) >= lax.broadcasted_iota(jnp.int32, (t, t), 1),
                     1.0, 0.0).astype(BF16)
    bcum = functools.reduce(jnp.add, [_dot(tril, part) for part in _split_bf16(jnp.log(f), 3)])
    causal = (lax.broadcasted_iota(jnp.int32, (C, HEAD_DIM), 1) <= lax.broadcasted_iota(jnp.int32, (C, HEAD_DIM), 0))
    zero_rows = jnp.zeros((C, HEAD_DIM), BF16)
    states = [st_scr[h] for h in range(heads)]
    tiles = []

    for c in range(t // C):
        rs = slice(c * C, (c + 1) * C)
        b = bcum[rs] if c == 0 else bcum[rs] - bcum[c * C - 1:c * C]
        b_last = b[C - 1:C]
        qt = (q[rs] * jnp.exp(b)).astype(BF16)
        kt = (kk[rs] * jnp.exp(-b)).astype(BF16)
        kh = (kk[rs] * jnp.exp(b_last - b)).astype(BF16)
        decay = jnp.exp(b_last)
        vc = v[rs]
        vb = vc.astype(BF16)
        sls = [slice(h * HEAD_DIM, (h + 1) * HEAD_DIM) for h in range(heads)]
        out1 = [_dot_nt(qt[:, sl], jnp.concatenate([st.astype(BF16), kt[:, sl], zero_rows], axis=0))
                for sl, st in zip(sls, states)]
        v_t = [jnp.concatenate([jnp.zeros((C, HEAD_DIM), F32), vc[:, sl]], axis=0).T.astype(BF16) for sl in sls]
        scores = [jnp.where(causal, o1[:, HEAD_DIM:], 0.0).astype(BF16) for o1 in out1]
        out2 = [_dot(jnp.concatenate([sc, vt], axis=0), jnp.concatenate([vb[:, sl], kh[:, sl]], axis=0))
                for sc, vt, sl in zip(scores, v_t, sls)]
        states = [st * decay[:, sl] + o2[C:] for st, o2, sl in zip(states, out2, sls)]
        tiles.append(jnp.concatenate([_rms(o1[:, :HEAD_DIM] + o2[:C], gain[:, sl])
                                      for o1, o2, sl in zip(out1, out2, sls)], axis=1))

    a_ref[...] = jnp.concatenate(tiles, axis=0)[:rows] * gate
    for h in range(heads):
        st_scr[h] = states[h]

    @pl.when(tt == pl.num_programs(1) - 1)
    def _():
        for h in range(heads):
            sout_ref[0, h] = st_scr[h].T


def _hgrn(proj, lb, gain, s0, batch, seq, t_tile):
    n = proj.shape[0]
    d = lb.shape[1]
    heads = d // HEAD_DIM
    n_t = seq // t_tile
    col_spec = lambda cb: pl.BlockSpec((t_tile, d), lambda b, t: (b * n_t + t, cb))
    state_spec = pl.BlockSpec((1, heads, HEAD_DIM, HEAD_DIM), lambda b, t: (b, 0, 0, 0))
    return pl.pallas_call(
        functools.partial(_hgrn_kernel, heads=heads),
        grid=(batch, n_t),
        in_specs=[col_spec(0), col_spec(1), col_spec(2), col_spec(3),
                  pl.BlockSpec((1, d), lambda b, t: (0, 0)),
                  pl.BlockSpec((1, d), lambda b, t: (0, 0)),
                  state_spec],
        out_specs=[pl.BlockSpec((t_tile, d), lambda b, t: (b * n_t + t, 0)), state_spec],
        out_shape=[jax.ShapeDtypeStruct((n, d), F32),
                   jax.ShapeDtypeStruct((batch, heads, HEAD_DIM, HEAD_DIM), F32)],
        scratch_shapes=[pltpu.VMEM((heads, HEAD_DIM, HEAD_DIM), F32)],
        compiler_params=_cparams(("parallel", "arbitrary")),
        name="hgrn",
    )(proj, proj, proj, proj, lb, gain, s0)


def _topk_bias(s, valid, topk):
    n_blocks = len(valid)
    rows = []
    for n in range(n_blocks):
        sn = s[n:n + 1, :]
        cnt = jnp.zeros(sn.shape, jnp.int32)
        for m in range(n_blocks):
            if m == n:
                continue
            sm = s[m:m + 1, :]
            beats = (sm >= sn) if m < n else (sm > sn)
            cnt = cnt + jnp.where(beats & valid[m], 1, 0)
        rows.append(jnp.where(cnt < topk, 0.0, -jnp.inf))
    return rows


def _moba_prompt_kernel(q_ref, k_ref, v_ref, o_ref, *, nb, blk):
    t = nb * blk
    q = q_ref[...]
    k = k_ref[...]
    qb = (q * SOFTMAX_LOG2_SCALE).astype(BF16)
    kb = k.astype(BF16)
    vt = v_ref[...].T.astype(BF16)

    kmean = jnp.concatenate(
        [jnp.mean(k[n * blk:(n + 1) * blk, :], axis=0, keepdims=True) for n in range(nb)]
        + [jnp.zeros((BF16_ROWS - nb % BF16_ROWS, HEAD_DIM), F32)], axis=0)
    s = _dot_nt_split(kmean, q)
    q_blk = lax.broadcasted_iota(jnp.int32, (1, t), 1) // blk
    bias = _topk_bias(s, [q_blk > m for m in range(nb - 1)], MOBA_TOPK)

    key = lax.broadcasted_iota(jnp.int32, (blk, blk), 0)
    qry = lax.broadcasted_iota(jnp.int32, (blk, blk), 1)
    logits = lambda j: _dot_nt(kb[:(j + 1) * blk, :], qb[j * blk:(j + 1) * blk, :])
    lg_next = logits(0)
    for j in range(nb):
        qs = slice(j * blk, (j + 1) * blk)
        lg = lg_next
        if j + 1 < nb:
            lg_next = logits(j + 1)
        parts = [lg[n * blk:(n + 1) * blk, :] + bias[n][:, qs] for n in range(j)]
        parts.append(jnp.where(key <= qry, lg[j * blk:, :], -jnp.inf))
        m = functools.reduce(jnp.maximum, [jnp.max(p, axis=0, keepdims=True) for p in parts])
        ps = [jnp.exp2(p - m) for p in parts]
        l = functools.reduce(jnp.add, [jnp.sum(p, axis=0, keepdims=True) for p in ps])
        pb = jnp.concatenate([p.astype(BF16) for p in ps], axis=0)
        o_ref[qs, :] = (_dot(vt[:, :(j + 1) * blk], pb) / l).T


def _moba_prompt(proj, k_new, v_new, batch, seq, q_col_block):
    n, d = k_new.shape
    heads = d // HEAD_DIM
    blk = MOBA_BLOCK
    nb = seq // blk
    seq_head = lambda cb: pl.BlockSpec((seq, HEAD_DIM), lambda b, h: (b, cb * heads + h))
    return pl.pallas_call(
        functools.partial(_moba_prompt_kernel, nb=nb, blk=blk),
        grid=(batch, heads),
        in_specs=[seq_head(q_col_block), seq_head(0), seq_head(0)],
        out_specs=seq_head(0),
        out_shape=jax.ShapeDtypeStruct((n, d), F32),
        compiler_params=_cparams(("parallel", "parallel")),
        name="moba_prompt",
    )(proj, k_new, v_new)


def _block_diag_queries(q, heads, rows):
    d = q.shape[1]
    tiled = jnp.concatenate([q] * heads, axis=0)
    r = lax.broadcasted_iota(jnp.int32, (heads * rows, d), 0) // rows
    c = lax.broadcasted_iota(jnp.int32, (heads * rows, d), 1) // HEAD_DIM
    return jnp.where(r == c, tiled, 0.0)


def _moba_sample_kernel(pt_ref, q_ref, kn_ref, vn_ref, *refs, heads, rows, page, pages_per_step):
    del pt_ref
    k_refs, v_refs = refs[:pages_per_step], refs[pages_per_step:2 * pages_per_step]
    o_ref, lg_scr, ksum_scr, kmean_scr, bias_scr, hmask_scr, m_scr, l_scr, acc_scr = refs[2 * pages_per_step:]
    phase, g = pl.program_id(1), pl.program_id(2)
    blocks = range(pages_per_step // 2)
    hr = heads * rows
    d = heads * HEAD_DIM
    nb = bias_scr.shape[0]
    head = lambda h: slice(h * HEAD_DIM, (h + 1) * HEAD_DIM)
    pos_head = lambda ref: ref[...].reshape(page * heads, HEAD_DIM).astype(BF16)
    side_by_side = lambda refs, i: jnp.concatenate([pos_head(refs[2 * i]), pos_head(refs[2 * i + 1])], axis=1)

    q = q_ref[...]
    qs = q * SOFTMAX_LOG2_SCALE

    @pl.when(phase == 0)
    def _():
        for i, k_ref in enumerate(k_refs):
            ksum_scr[g * pages_per_step + i] = jnp.sum(k_ref[...], axis=0)
        q_hd = jnp.concatenate([qs[:, head(h)] for h in range(heads)], axis=0).astype(BF16)
        zero = jnp.zeros_like(q_hd)
        q_pair = jnp.concatenate([jnp.concatenate([q_hd, zero], axis=1),
                                  jnp.concatenate([zero, q_hd], axis=1)], axis=0)
        for i in blocks:
            lg_scr[g * len(blocks) + i] = _dot_nt(q_pair, side_by_side(k_refs, i))

    @pl.when((phase == 1) & (g == 0))
    def _():
        for n in range(nb):
            kmean_scr[n * heads:(n + 1) * heads, :] = (ksum_scr[2 * n] + ksum_scr[2 * n + 1]) * (0.5 / page)
        s = jnp.concatenate([_dot_nt_split(q[:, head(h)], kmean_scr[pl.ds(h, nb, stride=heads), :], F32)
                             for h in range(heads)], axis=0)
        cnt = jnp.zeros((hr, nb), jnp.int32)
        idx = lax.broadcasted_iota(jnp.int32, (hr, nb), 1)
        for m in range(nb):
            sm = s[:, m:m + 1]
            beats = (sm > s) | ((sm == s) & (idx > m))
            cnt = cnt + jnp.where(beats, 1, 0)
        bias = jnp.where(cnt < MOBA_TOPK, 0.0, -jnp.inf)
        for n in range(nb):
            col = jnp.broadcast_to(bias[:, n:n + 1], (hr, HEAD_DIM))
            bias_scr[n] = jnp.concatenate([col, col], axis=0)
        r = lax.broadcasted_iota(jnp.int32, hmask_scr.shape, 0)
        c = lax.broadcasted_iota(jnp.int32, hmask_scr.shape, 1)
        hmask_scr[...] = jnp.where(c % heads == (r % hr) // rows, 0.0, -jnp.inf)

        qbd = _block_diag_queries(qs, heads, rows).astype(BF16)
        kn = jnp.concatenate([kn_ref[...], jnp.zeros((page - rows, d), F32)], axis=0)
        vn = jnp.concatenate([vn_ref[...], jnp.zeros((page - rows, d), F32)], axis=0)
        r_own = lax.broadcasted_iota(jnp.int32, (hr, page), 0) % rows
        c_own = lax.broadcasted_iota(jnp.int32, (hr, page), 1)
        lg_own = jnp.where(c_own <= r_own, _dot_nt(qbd, kn.astype(BF16)), -jnp.inf)
        m0 = jnp.max(lg_own, axis=1, keepdims=True)
        p_own = jnp.exp2(lg_own - m0)
        pv = _dot(p_own.astype(BF16), vn.astype(BF16))
        m_scr[...] = m0
        l_scr[...] = jnp.sum(p_own, axis=1, keepdims=True)
        acc_scr[...] = jnp.concatenate([pv[h * rows:(h + 1) * rows, head(h)] for h in range(heads)], axis=0)

    @pl.when(phase == 1)
    def _():
        hmask = hmask_scr[...]
        lgs = [lg_scr[g * len(blocks) + i] + hmask
               + jnp.concatenate([bias_scr[g * len(blocks) + i]] * heads, axis=1) for i in blocks]
        m_row = functools.reduce(jnp.maximum, [jnp.max(lg, axis=1, keepdims=True) for lg in lgs])
        m_old = m_scr[...]
        m_new = jnp.maximum(m_old, jnp.maximum(m_row[:hr], m_row[hr:]))
        alpha = jnp.exp2(m_old - m_new)
        m_pair = jnp.concatenate([m_new, m_new], axis=0)
        ps = [jnp.exp2(lg - m_pair) for lg in lgs]
        l_row = functools.reduce(jnp.add, [jnp.sum(p, axis=1, keepdims=True) for p in ps])
        pv = _dot(jnp.concatenate([p.astype(BF16) for p in ps], axis=1),
                  jnp.concatenate([side_by_side(v_refs, i) for i in blocks], axis=0))
        m_scr[...] = m_new
        l_scr[...] = alpha * l_scr[...] + l_row[:hr] + l_row[hr:]
        acc_scr[...] = alpha * acc_scr[...] + pv[:hr, :HEAD_DIM] + pv[hr:, HEAD_DIM:]

    @pl.when((phase == 1) & (g == pl.num_programs(2) - 1))
    def _():
        out = acc_scr[...] / l_scr[...]
        for h in range(heads):
            o_ref[:, head(h)] = out[h * rows:(h + 1) * rows, :]


def _moba_sample(proj, k_new, v_new, cache_k, cache_v, page_table, layer, rows, q_col_block):
    n, d = k_new.shape
    _, _, page, heads, dh = cache_k.shape
    batch, n_pages = page_table.shape
    assert dh == HEAD_DIM and 2 * page == MOBA_BLOCK and n_pages % 2 == 0
    nb = n_pages // 2
    hr = heads * rows
    pt = page_table.reshape(-1)

    pps = 2 * math.gcd(nb, 4)
    steps = n_pages // pps

    def page_spec(step_of, i):
        return pl.BlockSpec((None, None, page, heads, dh),
                            lambda b, ph, g, pt: (layer, pt[b * n_pages + step_of(ph, g) * pps + i], 0, 0, 0))

    k_step = lambda ph, g: jnp.where(ph == 0, g, steps - 1)
    v_step = lambda ph, g: jnp.where(ph == 0, 0, g)
    row_spec = lambda cb: pl.BlockSpec((rows, d), lambda b, ph, g, pt: (b, cb))
    return pl.pallas_call(
        functools.partial(_moba_sample_kernel, heads=heads, rows=rows, page=page, pages_per_step=pps),
        grid_spec=pltpu.PrefetchScalarGridSpec(
            num_scalar_prefetch=1,
            grid=(batch, 2, steps),
            in_specs=[row_spec(q_col_block), row_spec(0), row_spec(0)]
                     + [page_spec(k_step, i) for i in range(pps)] + [page_spec(v_step, i) for i in range(pps)],
            out_specs=row_spec(0),
            scratch_shapes=[pltpu.VMEM((nb, 2 * hr, page * heads), F32),
                            pltpu.VMEM((n_pages, heads, dh), F32),
                            pltpu.VMEM((nb * heads, dh), F32),
                            pltpu.VMEM((nb, 2 * hr, dh), F32),
                            pltpu.VMEM((2 * hr, page * heads), F32),
                            pltpu.VMEM((hr, 1), F32), pltpu.VMEM((hr, 1), F32), pltpu.VMEM((hr, dh), F32)],
        ),
        out_shape=jax.ShapeDtypeStruct((n, d), F32),
        compiler_params=_cparams(("parallel", "arbitrary", "arbitrary")),
        name="moba_sample",
    )(pt, proj, k_new, v_new, *([cache_k] * pps), *([cache_v] * pps))


def _merge_kernel(a_ref, b_ref, ga_ref, gb_ref, x_ref, w_ref, gpost_ref, gpre_ref, x1_ref, h2_ref):
    merged = _sigmoid(ga_ref[...]) * a_ref[...] + _sigmoid(gb_ref[...]) * b_ref[...]
    m = _dot(merged.astype(BF16), w_ref[...])
    x1 = x_ref[...] + _rms(m, gpost_ref[...])
    x1_ref[...] = x1
    h2_ref[...] = _rms(x1, gpre_ref[...]).astype(BF16)


def _merge(branch_a, branch_b, proj, x, w_out, g_post, g_pre, tm, ga_block, gb_block):
    n, d = x.shape
    row = lambda cb: pl.BlockSpec((tm, d), lambda i: (i, cb))
    vec = pl.BlockSpec((1, d), lambda i: (0, 0))
    return pl.pallas_call(
        _merge_kernel,
        grid=(n // tm,),
        in_specs=[row(0), row(0), row(ga_block), row(gb_block), row(0),
                  pl.BlockSpec((d, d), lambda i: (0, 0)), vec, vec],
        out_specs=[row(0), row(0)],
        out_shape=[jax.ShapeDtypeStruct((n, d), F32), jax.ShapeDtypeStruct((n, d), BF16)],
        compiler_params=_cparams(("parallel",)),
        name="merge",
    )(branch_a, branch_b, proj, proj, x, w_out, g_post, g_pre)


def _gelu_tanh(x):
    return 0.5 * x * (1.0 + jnp.tanh(math.sqrt(2.0 / math.pi) * (x + 0.044715 * (x * x * x))))


def _causal_conv3(up, prev2, prev1, t, cw, cb):
    p1 = jnp.where(t == 0, prev1, pltpu.roll(up, 1, 0))
    p2 = jnp.where(t == 0, prev2, jnp.where(t == 1, prev1, pltpu.roll(up, 2, 0)))
    return cb + cw[0:1, :] * p2 + cw[1:2, :] * p1 + cw[2:3, :] * up


def _ffn_prompt_kernel(h_ref, x_ref, wup_ref, cw_ref, cb_ref, wdn_ref, g_ref, y_ref, st_ref,
                       carry_scr, win_scr, act_scr, *, d_ff, cs):
    tm = h_ref.shape[0]
    h = h_ref[...]
    halo = carry_scr.shape[0]

    @pl.when(pl.program_id(1) == 0)
    def _():
        carry_scr[...] = jnp.zeros_like(carry_scr)

    def conv(col0, win):
        cols = slice(col0, col0 + cs)
        up = _dot(h, wup_ref[:, cols])
        win[0:halo, :] = carry_scr[:, cols]
        win[halo:, :] = up
        carry_scr[:, cols] = up[tm - halo:, :]
        st_ref[0, :, cols] = up[tm - (CONV_W - 1):, :]
        cw = cw_ref[:, cols]
        return (cb_ref[:, cols] + cw[0:1, :] * win[halo - 2:halo - 2 + tm, :]
                + cw[1:2, :] * win[halo - 1:halo - 1 + tm, :] + cw[2:3, :] * up)

    for c in range(d_ff // cs):
        gate = conv(c * cs, win_scr.at[(2 * c) % win_scr.shape[0]])
        lin = conv(d_ff + c * cs, win_scr.at[(2 * c + 1) % win_scr.shape[0]])
        act_scr[:, c * cs:(c + 1) * cs] = (_gelu_tanh(gate) * lin).astype(BF16)
    y_ref[...] = x_ref[...] + _rms(_dot(act_scr[...], wdn_ref[...]), g_ref[...])


def _ffn_sample_kernel(h_ref, x_ref, wg_ref, wl_ref, cwg_ref, cwl_ref, cbg_ref, cbl_ref, wdn_ref, g_ref,
                       s0g_ref, s0l_ref, s1g_ref, s1l_ref, y_ref, upg_ref, upl_ref, acc_scr, *, seq_rows):
    c = pl.program_id(0)
    h = h_ref[...]
    t = lax.broadcasted_iota(jnp.int32, upg_ref.shape, 0) % seq_rows

    @pl.when(c == 0)
    def _():
        acc_scr[...] = jnp.zeros_like(acc_scr)

    upg = _dot(h, wg_ref[...])
    upl = _dot(h, wl_ref[...])
    upg_ref[...] = upg
    upl_ref[...] = upl
    gate = _causal_conv3(upg, s0g_ref[...], s1g_ref[...], t, cwg_ref[...], cbg_ref[...])
    lin = _causal_conv3(upl, s0l_ref[...], s1l_ref[...], t, cwl_ref[...], cbl_ref[...])
    acc_scr[...] += _dot((_gelu_tanh(gate) * lin).astype(BF16), wdn_ref[...])

    @pl.when(c == pl.num_programs(0) - 1)
    def _():
        y_ref[...] = x_ref[...] + _rms(acc_scr[...], g_ref[...])


def _ffn_cs(d_ff):
    for cs in (512, 256, 128):
        if d_ff % cs == 0:
            return cs
    raise ValueError(f"d_ff={d_ff} is not a multiple of the 128-lane tile")


def _ffn_prompt(h2, x1, w_up, conv_w, conv_b, w_down, gain, batch, seq, tm):
    n, d = x1.shape
    d_ff = w_down.shape[0]
    n_t = seq // tm
    full = lambda a: pl.BlockSpec(a.shape, lambda b, t: (0, 0))
    row = pl.BlockSpec((tm, d), lambda b, t: (b * n_t + t, 0))
    cs = _ffn_cs(d_ff)
    halo = 8
    return pl.pallas_call(
        functools.partial(_ffn_prompt_kernel, d_ff=d_ff, cs=cs),
        grid=(batch, n_t),
        in_specs=[row, row, full(w_up), full(conv_w), full(conv_b), full(w_down), full(gain)],
        out_specs=[row, pl.BlockSpec((1, CONV_W - 1, 2 * d_ff), lambda b, t: (b, 0, 0))],
        out_shape=[jax.ShapeDtypeStruct((n, d), F32),
                   jax.ShapeDtypeStruct((batch, CONV_W - 1, 2 * d_ff), F32)],
        scratch_shapes=[pltpu.VMEM((halo, 2 * d_ff), F32), pltpu.VMEM((4, halo + tm, cs), F32),
                        pltpu.VMEM((tm, d_ff), BF16)],
        compiler_params=_cparams(("parallel", "arbitrary")),
        name="ffn_prompt",
    )(h2, x1, w_up, conv_w, conv_b, w_down, gain)


def _ffn_sample(h2, x1, w_up, conv_w, conv_b, w_down, gain, state, seq):
    n, d = x1.shape
    d_ff = w_down.shape[0]
    batch = n // seq
    s0 = jnp.broadcast_to(state[:, 0:1, :], (batch, seq, 2 * d_ff)).reshape(n, 2 * d_ff)
    s1 = jnp.broadcast_to(state[:, 1:2, :], (batch, seq, 2 * d_ff)).reshape(n, 2 * d_ff)
    cs = _ffn_cs(d_ff)
    nc = d_ff // cs
    full = lambda a: pl.BlockSpec(a.shape, lambda c: (0, 0))
    gate = lambda rows: pl.BlockSpec((rows, cs), lambda c: (0, c))
    lin = lambda rows: pl.BlockSpec((rows, cs), lambda c: (0, nc + c))
    y, upg, upl = pl.pallas_call(
        functools.partial(_ffn_sample_kernel, seq_rows=seq),
        grid=(nc,),
        in_specs=[full(h2), full(x1), gate(d), lin(d), gate(CONV_W), lin(CONV_W), gate(1), lin(1),
                  pl.BlockSpec((cs, d), lambda c: (c, 0)), full(gain), gate(n), lin(n), gate(n), lin(n)],
        out_specs=[full(x1), gate(n), gate(n)],
        out_shape=[jax.ShapeDtypeStruct((n, d), F32), jax.ShapeDtypeStruct((n, d_ff), F32),
                   jax.ShapeDtypeStruct((n, d_ff), F32)],
        scratch_shapes=[pltpu.VMEM((n, d), F32)],
        compiler_params=_cparams(("arbitrary",)),
        name="ffn_sample",
    )(h2, x1, w_up, w_up, conv_w, conv_w, conv_b, conv_b, w_down, gain, s0, s0, s1, s1)
    up = jnp.concatenate([upg, upl], axis=1)
    return y, up.reshape(batch, seq, 2 * d_ff)[:, seq - (CONV_W - 1):, :]


def _row_tile(n, cap):
    t = min(n, cap)
    while n % t:
        t //= 2
    return t


def kernel(x_prompt, x_sample, cache_k, cache_v, page_table, state_hgrn, state_ffn_conv, norm_mix_pre, norm_mix_post, w_in, hgrn_lb_logits, hgrn_norm, w_out, norm_ffn_pre, norm_ffn_post, w_up, conv_w, conv_b, w_down):
    bp, tp, d = x_prompt.shape
    bs, ts, _ = x_sample.shape
    depth = w_in.shape[0]
    heads = d // HEAD_DIM
    n_pool, page = cache_k.shape[1], cache_k.shape[2]
    lb_all = jnp.cumsum(jax.nn.softmax(hgrn_lb_logits.astype(F32), axis=0), axis=0)

    xp = x_prompt.reshape(bp * tp, d)
    xs = x_sample.reshape(bs * ts, d)
    outs = [[] for _ in range(8)]
    for l in range(depth):
        wl = w_in[l]
        w_perm = jnp.concatenate([wl[:, :5 * d], wl[:, 7 * d:], wl[:, 5 * d:7 * d]], axis=1).astype(BF16)
        w_out_b = w_out[l].astype(BF16)
        w_up_b = w_up[l].astype(BF16)
        w_down_b = w_down[l].astype(BF16)
        vec = lambda a: a[l].reshape(1, -1)
        lb = lb_all[l].reshape(1, d)
        Q_BLK, GA_BLK, GB_BLK = 4, 5, 6

        def mixers(x, batch, seq, s0, moba):
            proj, k_new, v_new = _inproj(x, vec(norm_mix_pre), w_perm, _row_tile(x.shape[0], 1024))
            branch_a, s_new = _hgrn(proj, lb, vec(hgrn_norm), s0, batch, seq, _row_tile(seq, 256))
            branch_b = moba(proj, k_new, v_new)
            x1, h2 = _merge(branch_a, branch_b, proj, x, w_out_b, vec(norm_mix_post), vec(norm_ffn_pre),
                            _row_tile(x.shape[0], 512), GA_BLK, GB_BLK)
            return x1, h2, k_new, v_new, s_new

        s0p = jnp.zeros((bp, heads, HEAD_DIM, HEAD_DIM), F32)
        x1, h2, kp, vp, sp = mixers(xp, bp, tp, s0p,
                                    lambda pr, k, v: _moba_prompt(pr, k, v, bp, tp, Q_BLK))
        xp, cp = _ffn_prompt(h2, x1, w_up_b, conv_w[l], conv_b[l].reshape(1, -1), w_down_b,
                             vec(norm_ffn_post), bp, tp, _row_tile(tp, 512))
        x1, h2, kn, vn, sn = mixers(xs, bs, ts, state_hgrn[l],
                                    lambda pr, k, v: _moba_sample(pr, k, v, cache_k, cache_v, page_table, l, ts, Q_BLK))
        xs, cn = _ffn_sample(h2, x1, w_up_b, conv_w[l], conv_b[l].reshape(1, -1), w_down_b,
                             vec(norm_ffn_post), state_ffn_conv[l], ts)
        kv = lambda a, b, t: a.reshape(b, t, heads, HEAD_DIM)
        for o, val in zip(outs, (kv(kp, bp, tp), kv(vp, bp, tp), sp, cp, kv(kn, bs, ts), kv(vn, bs, ts), sn, cn)):
            o.append(val)
    return (xp.reshape(bp, tp, d), xs.reshape(bs, ts, d), *(jnp.stack(o) for o in outs))
```

```python
import functools
import math

import jax
import jax.numpy as jnp
from jax import lax
from jax.experimental import pallas as pl
from jax.experimental.pallas import tpu as pltpu

F32 = jnp.float32
BF16 = jnp.bfloat16

NORM_EPS = 1e-6
HEAD_DIM = 128
MOBA_BLOCK = 256
MOBA_TOPK = 3
HGRN_CHUNK = 64
CONV_W = 3
BF16_ROWS = 16
SOFTMAX_LOG2_SCALE = math.log2(math.e) / math.sqrt(HEAD_DIM)
VMEM_LIMIT = 56 * 1024 * 1024


def _cparams(sem):
    return pltpu.CompilerParams(dimension_semantics=sem, vmem_limit_bytes=VMEM_LIMIT)


def _sigmoid(x):
    return 1.0 / (1.0 + jnp.exp(-x))


def _rms(x, gain):
    return x * lax.rsqrt(jnp.mean(x * x, axis=-1, keepdims=True) + NORM_EPS) * gain


def _dot(a, b):
    return jnp.dot(a, b, preferred_element_type=F32)


def _dot_nt(a, b):
    return lax.dot_general(a, b, (((1,), (1,)), ((), ())), preferred_element_type=F32)


def _dot_tn(a, b):
    return lax.dot_general(a, b, (((0,), (0,)), ((), ())), preferred_element_type=F32)


def _split_bf16(x, parts):
    out = []
    for _ in range(parts):
        p = x.astype(BF16)
        out.append(p)
        x = x - p.astype(F32)
    return out


def _dot_nt_split(a, b, operand_dtype=BF16):
    ah, al = (p.astype(operand_dtype) for p in _split_bf16(a, 2))
    bh, bl = (p.astype(operand_dtype) for p in _split_bf16(b, 2))
    return _dot_nt(ah, bh) + _dot_nt(ah, bl) + _dot_nt(al, bh)


def _inproj_kernel(x_ref, g_ref, w_ref, p_ref, k_ref, v_ref, h_scr, *, n_main):
    j = pl.program_id(1)

    @pl.when(j == 0)
    def _():
        h_scr[...] = _rms(x_ref[...], g_ref[...]).astype(BF16)

    @pl.when(j < n_main)
    def _():
        p_ref[...] = _dot(h_scr[...], w_ref[...])

    @pl.when(j == n_main)
    def _():
        k_ref[...] = _dot(h_scr[...], w_ref[...])

    @pl.when(j == n_main + 1)
    def _():
        v_ref[...] = _dot(h_scr[...], w_ref[...])


def _inproj(x, gain, w_perm, tm):
    n, d = x.shape
    n_blocks = w_perm.shape[1] // d
    n_main = n_blocks - 2
    return pl.pallas_call(
        functools.partial(_inproj_kernel, n_main=n_main),
        grid=(n // tm, n_blocks),
        in_specs=[
            pl.BlockSpec((tm, d), lambda i, j: (i, 0)),
            pl.BlockSpec((1, d), lambda i, j: (0, 0)),
            pl.BlockSpec((d, d), lambda i, j: (0, j)),
        ],
        out_specs=[
            pl.BlockSpec((tm, d), lambda i, j: (i, jnp.minimum(j, n_main - 1))),
            pl.BlockSpec((tm, d), lambda i, j: (i, 0)),
            pl.BlockSpec((tm, d), lambda i, j: (i, 0)),
        ],
        out_shape=[
            jax.ShapeDtypeStruct((n, n_main * d), F32),
            jax.ShapeDtypeStruct((n, d), F32),
            jax.ShapeDtypeStruct((n, d), F32),
        ],
        scratch_shapes=[pltpu.VMEM((tm, d), BF16)],
        compiler_params=_cparams(("parallel", "arbitrary")),
        name="inproj",
    )(x, gain, w_perm)


def _hgrn_kernel(hq_ref, hf_ref, hi_ref, hg_ref, ga_ref, lb_ref, gain_ref, s0_ref, a_ref, sout_ref, st_scr,
                 *, heads):
    tt = pl.program_id(1)
    C = HGRN_CHUNK
    assert 2 * C == HEAD_DIM
    rows = hq_ref.shape[0]
    d = heads * HEAD_DIM
    pad = (-rows) % C
    t = rows + pad

    @pl.when(tt == 0)
    def _():
        for h in range(heads):
            st_scr[h] = s0_ref[0, h].T

    def padded(x, fill):
        return jnp.concatenate([x, jnp.full((pad, d), fill, F32)], axis=0) if pad else x

    lb = lb_ref[...]
    f = padded(lb + (1.0 - lb) * _sigmoid(hf_ref[...]), 1.0)
    kk = 1.0 - f
    hq = hq_ref[...]
    q = padded(hq * _sigmoid(hq), 0.0)
    v = padded(hi_ref[...], 0.0)
    hg = hg_ref[...]
    gate = hg * _sigmoid(hg)
    gain = gain_ref[...]
    tril = jnp.where(lax.broadcasted_iota(jnp.int32, (t, t), 0) >= lax.broadcasted_iota(jnp.int32, (t, t), 1),
                     1.0, 0.0).astype(BF16)
    bcum = functools.reduce(jnp.add, [_dot(tril, part) for part in _split_bf16(jnp.log(f), 3)])
    causal = (lax.broadcasted_iota(jnp.int32, (C, HEAD_DIM), 1) <= lax.broadcasted_iota(jnp.int32, (C, HEAD_DIM), 0))
    zero_rows = jnp.zeros((C, HEAD_DIM), BF16)
    states = [st_scr[h] for h in range(heads)]
    tiles = []

    for c in range(t // C):
        rs = slice(c * C, (c + 1) * C)
        b = bcum[rs] if c == 0 else bcum[rs] - bcum[c * C - 1:c * C]
        b_last = b[C - 1:C]
        qt = (q[rs] * jnp.exp(b)).astype(BF16)
        kt = (kk[rs] * jnp.exp(-b)).astype(BF16)
        kh = (kk[rs] * jnp.exp(b_last - b)).astype(BF16)
        decay = jnp.exp(b_last)
        vc = v[rs]
        vb = vc.astype(BF16)
        sls = [slice(h * HEAD_DIM, (h + 1) * HEAD_DIM) for h in range(heads)]
        out1 = [_dot_nt(qt[:, sl], jnp.concatenate([st.astype(BF16), kt[:, sl], zero_rows], axis=0))
                for sl, st in zip(sls, states)]
        v_t = [jnp.concatenate([jnp.zeros((C, HEAD_DIM), F32), vc[:, sl]], axis=0).T.astype(BF16) for sl in sls]
        scores = [jnp.where(causal, o1[:, HEAD_DIM:], 0.0).astype(BF16) for o1 in out1]
        out2 = [_dot(jnp.concatenate([sc, vt], axis=0), jnp.concatenate([vb[:, sl], kh[:, sl]], axis=0))
                for sc, vt, sl in zip(scores, v_t, sls)]
        states = [st * decay[:, sl] + o2[C:] for st, o2, sl in zip(states, out2, sls)]
        tiles.append(jnp.concatenate([_rms(o1[:, :HEAD_DIM] + o2[:C], gain[:, sl])
                                      for o1, o2, sl in zip(out1, out2, sls)], axis=1))

    a_ref[...] = _sigmoid(ga_ref[...]) * (jnp.concatenate(tiles, axis=0)[:rows] * gate)
    for h in range(heads):
        st_scr[h] = states[h]

    @pl.when(tt == pl.num_programs(1) - 1)
    def _():
        for h in range(heads):
            sout_ref[0, h] = st_scr[h].T


def _hgrn(proj, lb, gain, s0, batch, seq, t_tile, ga_block):
    n = proj.shape[0]
    d = lb.shape[1]
    heads = d // HEAD_DIM
    n_t = seq // t_tile
    col_spec = lambda cb: pl.BlockSpec((t_tile, d), lambda b, t: (b * n_t + t, cb))
    state_spec = pl.BlockSpec((1, heads, HEAD_DIM, HEAD_DIM), lambda b, t: (b, 0, 0, 0))
    return pl.pallas_call(
        functools.partial(_hgrn_kernel, heads=heads),
        grid=(batch, n_t),
        in_specs=[col_spec(0), col_spec(1), col_spec(2), col_spec(3), col_spec(ga_block),
                  pl.BlockSpec((1, d), lambda b, t: (0, 0)),
                  pl.BlockSpec((1, d), lambda b, t: (0, 0)),
                  state_spec],
        out_specs=[pl.BlockSpec((t_tile, d), lambda b, t: (b * n_t + t, 0)), state_spec],
        out_shape=[jax.ShapeDtypeStruct((n, d), F32),
                   jax.ShapeDtypeStruct((batch, heads, HEAD_DIM, HEAD_DIM), F32)],
        scratch_shapes=[pltpu.VMEM((heads, HEAD_DIM, HEAD_DIM), F32)],
        compiler_params=_cparams(("parallel", "arbitrary")),
        name="hgrn",
    )(proj, proj, proj, proj, proj, lb, gain, s0)


def _topk_bias(s, valid, topk):
    n_blocks = len(valid)
    rows = []
    for n in range(n_blocks):
        sn = s[n:n + 1, :]
        cnt = jnp.zeros(sn.shape, jnp.int32)
        for m in range(n_blocks):
            if m == n:
                continue
            sm = s[m:m + 1, :]
            beats = (sm >= sn) if m < n else (sm > sn)
            cnt = cnt + jnp.where(beats & valid[m], 1, 0)
        rows.append(jnp.where(cnt < topk, 0.0, -jnp.inf))
    return rows


def _moba_prompt_kernel(q_ref, k_ref, v_ref, gb_ref, o_ref, *, nb, blk):
    t = nb * blk
    q = q_ref[...]
    k = k_ref[...]
    qb = (q * SOFTMAX_LOG2_SCALE).astype(BF16)
    kb = k.astype(BF16)
    vt = v_ref[...].T.astype(BF16)

    kmean = jnp.concatenate(
        [jnp.mean(k[n * blk:(n + 1) * blk, :], axis=0, keepdims=True) for n in range(nb)]
        + [jnp.zeros((BF16_ROWS - nb % BF16_ROWS, HEAD_DIM), F32)], axis=0)
    s = _dot_nt_split(kmean, q)
    q_blk = lax.broadcasted_iota(jnp.int32, (1, t), 1) // blk
    bias = _topk_bias(s, [q_blk > m for m in range(nb - 1)], MOBA_TOPK)

    key = lax.broadcasted_iota(jnp.int32, (blk, blk), 0)
    qry = lax.broadcasted_iota(jnp.int32, (blk, blk), 1)
    logits = lambda j: _dot_nt(kb[:(j + 1) * blk, :], qb[j * blk:(j + 1) * blk, :])
    lg_next = logits(0)
    for j in range(nb):
        qs = slice(j * blk, (j + 1) * blk)
        lg = lg_next
        if j + 1 < nb:
            lg_next = logits(j + 1)
        parts = [lg[n * blk:(n + 1) * blk, :] + bias[n][:, qs] for n in range(j)]
        parts.append(jnp.where(key <= qry, lg[j * blk:, :], -jnp.inf))
        m = functools.reduce(jnp.maximum, [jnp.max(p, axis=0, keepdims=True) for p in parts])
        ps = [jnp.exp2(p - m) for p in parts]
        l = functools.reduce(jnp.add, [jnp.sum(p, axis=0, keepdims=True) for p in ps])
        pb = jnp.concatenate([p.astype(BF16) for p in ps], axis=0)
        o_ref[qs, :] = _sigmoid(gb_ref[qs, :]) * (_dot(vt[:, :(j + 1) * blk], pb) / l).T


def _moba_prompt(proj, k_new, v_new, batch, seq, q_col_block, gb_col_block):
    n, d = k_new.shape
    heads = d // HEAD_DIM
    blk = MOBA_BLOCK
    nb = seq // blk
    seq_head = lambda cb: pl.BlockSpec((seq, HEAD_DIM), lambda b, h: (b, cb * heads + h))
    return pl.pallas_call(
        functools.partial(_moba_prompt_kernel, nb=nb, blk=blk),
        grid=(batch, heads),
        in_specs=[seq_head(q_col_block), seq_head(0), seq_head(0), seq_head(gb_col_block)],
        out_specs=seq_head(0),
        out_shape=jax.ShapeDtypeStruct((n, d), F32),
        compiler_params=_cparams(("parallel", "parallel")),
        name="moba_prompt",
    )(proj, k_new, v_new, proj)


def _block_diag_queries(q, heads, rows):
    d = q.shape[1]
    tiled = jnp.concatenate([q] * heads, axis=0)
    r = lax.broadcasted_iota(jnp.int32, (heads * rows, d), 0) // rows
    c = lax.broadcasted_iota(jnp.int32, (heads * rows, d), 1) // HEAD_DIM
    return jnp.where(r == c, tiled, 0.0)


def _moba_sample_kernel(pt_ref, q_ref, kn_ref, vn_ref, gb_ref, *refs, heads, rows, page, pages_per_step):
    del pt_ref
    k_refs, v_refs = refs[:pages_per_step], refs[pages_per_step:2 * pages_per_step]
    o_ref, lg_scr, ksum_scr, kmean_scr, bias_scr, hmask_scr, m_scr, l_scr, acc_scr = refs[2 * pages_per_step:]
    phase, g = pl.program_id(1), pl.program_id(2)
    blocks = range(pages_per_step // 2)
    hr = heads * rows
    d = heads * HEAD_DIM
    nb = bias_scr.shape[0]
    head = lambda h: slice(h * HEAD_DIM, (h + 1) * HEAD_DIM)
    pos_head = lambda ref: ref[...].reshape(page * heads, HEAD_DIM).astype(BF16)
    side_by_side = lambda refs, i: jnp.concatenate([pos_head(refs[2 * i]), pos_head(refs[2 * i + 1])], axis=1)

    q = q_ref[...]
    qs = q * SOFTMAX_LOG2_SCALE

    @pl.when(phase == 0)
    def _():
        for i, k_ref in enumerate(k_refs):
            ksum_scr[g * pages_per_step + i] = jnp.sum(k_ref[...], axis=0)
        q_hd = jnp.concatenate([qs[:, head(h)] for h in range(heads)], axis=0).astype(BF16)
        zero = jnp.zeros_like(q_hd)
        q_pair = jnp.concatenate([jnp.concatenate([q_hd, zero], axis=1),
                                  jnp.concatenate([zero, q_hd], axis=1)], axis=0)
        for i in blocks:
            lg_scr[g * len(blocks) + i] = _dot_nt(q_pair, side_by_side(k_refs, i))

    @pl.when((phase == 1) & (g == 0))
    def _():
        for n in range(nb):
            kmean_scr[n * heads:(n + 1) * heads, :] = (ksum_scr[2 * n] + ksum_scr[2 * n + 1]) * (0.5 / page)
        s = jnp.concatenate([_dot_nt_split(q[:, head(h)], kmean_scr[pl.ds(h, nb, stride=heads), :], F32)
                             for h in range(heads)], axis=0)
        cnt = jnp.zeros((hr, nb), jnp.int32)
        idx = lax.broadcasted_iota(jnp.int32, (hr, nb), 1)
        for m in range(nb):
            sm = s[:, m:m + 1]
            beats = (sm > s) | ((sm == s) & (idx > m))
            cnt = cnt + jnp.where(beats, 1, 0)
        bias = jnp.where(cnt < MOBA_TOPK, 0.0, -jnp.inf)
        for n in range(nb):
            col = jnp.broadcast_to(bias[:, n:n + 1], (hr, HEAD_DIM))
            bias_scr[n] = jnp.concatenate([col, col], axis=0)
        r = lax.broadcasted_iota(jnp.int32, hmask_scr.shape, 0)
        c = lax.broadcasted_iota(jnp.int32, hmask_scr.shape, 1)
        hmask_scr[...] = jnp.where(c % heads == (r % hr) // rows, 0.0, -jnp.inf)

        qbd = _block_diag_queries(qs, heads, rows).astype(BF16)
        kn = jnp.concatenate([kn_ref[...], jnp.zeros((page - rows, d), F32)], axis=0)
        vn = jnp.concatenate([vn_ref[...], jnp.zeros((page - rows, d), F32)], axis=0)
        r_own = lax.broadcasted_iota(jnp.int32, (hr, page), 0) % rows
        c_own = lax.broadcasted_iota(jnp.int32, (hr, page), 1)
        lg_own = jnp.where(c_own <= r_own, _dot_nt(qbd, kn.astype(BF16)), -jnp.inf)
        m0 = jnp.max(lg_own, axis=1, keepdims=True)
        p_own = jnp.exp2(lg_own - m0)
        pv = _dot(p_own.astype(BF16), vn.astype(BF16))
        m_scr[...] = m0
        l_scr[...] = jnp.sum(p_own, axis=1, keepdims=True)
        acc_scr[...] = jnp.concatenate([pv[h * rows:(h + 1) * rows, head(h)] for h in range(heads)], axis=0)

    @pl.when(phase == 1)
    def _():
        hmask = hmask_scr[...]
        lgs = [lg_scr[g * len(blocks) + i] + hmask
               + jnp.concatenate([bias_scr[g * len(blocks) + i]] * heads, axis=1) for i in blocks]
        m_row = functools.reduce(jnp.maximum, [jnp.max(lg, axis=1, keepdims=True) for lg in lgs])
        m_old = m_scr[...]
        m_new = jnp.maximum(m_old, jnp.maximum(m_row[:hr], m_row[hr:]))
        alpha = jnp.exp2(m_old - m_new)
        m_pair = jnp.concatenate([m_new, m_new], axis=0)
        ps = [jnp.exp2(lg - m_pair) for lg in lgs]
        l_row = functools.reduce(jnp.add, [jnp.sum(p, axis=1, keepdims=True) for p in ps])
        pv = _dot(jnp.concatenate([p.astype(BF16) for p in ps], axis=1),
                  jnp.concatenate([side_by_side(v_refs, i) for i in blocks], axis=0))
        m_scr[...] = m_new
        l_scr[...] = alpha * l_scr[...] + l_row[:hr] + l_row[hr:]
        acc_scr[...] = alpha * acc_scr[...] + pv[:hr, :HEAD_DIM] + pv[hr:, HEAD_DIM:]

    @pl.when((phase == 1) & (g == pl.num_programs(2) - 1))
    def _():
        out = acc_scr[...] / l_scr[...]
        for h in range(heads):
            o_ref[:, head(h)] = _sigmoid(gb_ref[:, head(h)]) * out[h * rows:(h + 1) * rows, :]


def _moba_sample(proj, k_new, v_new, cache_k, cache_v, page_table, layer, rows, q_col_block, gb_col_block):
    n, d = k_new.shape
    _, _, page, heads, dh = cache_k.shape
    batch, n_pages = page_table.shape
    assert dh == HEAD_DIM and 2 * page == MOBA_BLOCK and n_pages % 2 == 0
    nb = n_pages // 2
    hr = heads * rows
    pt = page_table.reshape(-1)

    pps = 2 * math.gcd(nb, 4)
    steps = n_pages // pps

    def page_spec(step_of, i):
        return pl.BlockSpec((None, None, page, heads, dh),
                            lambda b, ph, g, pt: (layer, pt[b * n_pages + step_of(ph, g) * pps + i], 0, 0, 0))

    k_step = lambda ph, g: jnp.where(ph == 0, g, steps - 1)
    v_step = lambda ph, g: jnp.where(ph == 0, 0, g)
    row_spec = lambda cb: pl.BlockSpec((rows, d), lambda b, ph, g, pt: (b, cb))
    return pl.pallas_call(
        functools.partial(_moba_sample_kernel, heads=heads, rows=rows, page=page, pages_per_step=pps),
        grid_spec=pltpu.PrefetchScalarGridSpec(
            num_scalar_prefetch=1,
            grid=(batch, 2, steps),
            in_specs=[row_spec(q_col_block), row_spec(0), row_spec(0), row_spec(gb_col_block)]
                     + [page_spec(k_step, i) for i in range(pps)] + [page_spec(v_step, i) for i in range(pps)],
            out_specs=row_spec(0),
            scratch_shapes=[pltpu.VMEM((nb, 2 * hr, page * heads), F32),
                            pltpu.VMEM((n_pages, heads, dh), F32),
                            pltpu.VMEM((nb * heads, dh), F32),
                            pltpu.VMEM((nb, 2 * hr, dh), F32),
                            pltpu.VMEM((2 * hr, page * heads), F32),
                            pltpu.VMEM((hr, 1), F32), pltpu.VMEM((hr, 1), F32), pltpu.VMEM((hr, dh), F32)],
        ),
        out_shape=jax.ShapeDtypeStruct((n, d), F32),
        compiler_params=_cparams(("parallel", "arbitrary", "arbitrary")),
        name="moba_sample",
    )(pt, proj, k_new, v_new, proj, *([cache_k] * pps), *([cache_v] * pps))


def _merge_rows(gated_a, gated_b, x, w_out, g_post, g_pre):
    m = _dot((gated_a + gated_b).astype(BF16), w_out)
    x1 = x + _rms(m, g_post)
    return x1, _rms(x1, g_pre).astype(BF16)


def _merge_kernel(a_ref, b_ref, x_ref, w_ref, gpost_ref, gpre_ref, x1_ref, h2_ref):
    x1_ref[...], h2_ref[...] = _merge_rows(a_ref[...], b_ref[...], x_ref[...], w_ref[...],
                                           gpost_ref[...], gpre_ref[...])


def _merge(gated_a, gated_b, x, w_out, g_post, g_pre, tm):
    n, d = x.shape
    row = pl.BlockSpec((tm, d), lambda i: (i, 0))
    vec = pl.BlockSpec((1, d), lambda i: (0, 0))
    return pl.pallas_call(
        _merge_kernel,
        grid=(n // tm,),
        in_specs=[row, row, row, pl.BlockSpec((d, d), lambda i: (0, 0)), vec, vec],
        out_specs=[row, row],
        out_shape=[jax.ShapeDtypeStruct((n, d), F32), jax.ShapeDtypeStruct((n, d), BF16)],
        compiler_params=_cparams(("parallel",)),
        name="merge",
    )(gated_a, gated_b, x, w_out, g_post, g_pre)


def _gelu_tanh(x):
    return 0.5 * x * (1.0 + jnp.tanh(math.sqrt(2.0 / math.pi) * (x + 0.044715 * (x * x * x))))


def _ffn_prompt_kernel(a_ref, b_ref, x_ref, wout_ref, gpost_ref, gpre_ref, wup_ref, cw_ref, cb_ref, wdn_ref,
                       g_ref, y_ref, st_ref, carry_scr, win_scr, act_scr, *, d_ff, cs):
    tm = x_ref.shape[0]
    x1, h = _merge_rows(a_ref[...], b_ref[...], x_ref[...], wout_ref[...], gpost_ref[...], gpre_ref[...])
    halo = carry_scr.shape[0]

    @pl.when(pl.program_id(1) == 0)
    def _():
        carry_scr[...] = jnp.zeros_like(carry_scr)

    def conv(col0, win):
        cols = slice(col0, col0 + cs)
        up = _dot(h, wup_ref[:, cols])
        win[0:halo, :] = carry_scr[:, cols]
        win[halo:, :] = up
        carry_scr[:, cols] = up[tm - halo:, :]
        st_ref[0, :, cols] = up[tm - (CONV_W - 1):, :]
        cw = cw_ref[:, cols]
        return (cb_ref[:, cols] + cw[0:1, :] * win[halo - 2:halo - 2 + tm, :]
                + cw[1:2, :] * win[halo - 1:halo - 1 + tm, :] + cw[2:3, :] * up)

    for c in range(d_ff // cs):
        gate = conv(c * cs, win_scr.at[(2 * c) % win_scr.shape[0]])
        lin = conv(d_ff + c * cs, win_scr.at[(2 * c + 1) % win_scr.shape[0]])
        act_scr[:, c * cs:(c + 1) * cs] = (_gelu_tanh(gate) * lin).astype(BF16)
    y_ref[...] = x1 + _rms(_dot(act_scr[...], wdn_ref[...]), g_ref[...])


def _ffn_sample_kernel(h_ref, x_ref, wg_ref, wl_ref, cwg_ref, cwl_ref, cbg_ref, cbl_ref, wdn_ref, g_ref,
                       sg_ref, sl_ref, y_ref, ng_ref, nl_ref, acc_scr):
    c = pl.program_id(0)
    h = h_ref[...]
    batch, _, cs = sg_ref.shape
    seq_rows = h.shape[0] // batch
    t = lax.broadcasted_iota(jnp.int32, (batch, seq_rows, cs), 1)

    @pl.when(c == 0)
    def _():
        acc_scr[...] = jnp.zeros_like(acc_scr)

    def conv(w_ref, s_ref, n_ref, cw_ref, cb_ref):
        up = _dot(h, w_ref[...])
        by_seq = lambda a: a.reshape(batch, seq_rows, cs)
        n_ref[...] = by_seq(up)[:, seq_rows - (CONV_W - 1):, :]
        prev2, prev1 = s_ref[:, 0:1, :], s_ref[:, 1:2, :]
        p1 = jnp.where(t == 0, prev1, by_seq(pltpu.roll(up, 1, 0)))
        p2 = jnp.where(t == 0, prev2, jnp.where(t == 1, prev1, by_seq(pltpu.roll(up, 2, 0))))
        cw = cw_ref[...]
        return (cb_ref[...] + cw[0:1, :] * p2 + cw[1:2, :] * p1 + cw[2:3, :] * by_seq(up)).reshape(up.shape)

    gate = conv(wg_ref, sg_ref, ng_ref, cwg_ref, cbg_ref)
    lin = conv(wl_ref, sl_ref, nl_ref, cwl_ref, cbl_ref)
    acc_scr[...] += _dot((_gelu_tanh(gate) * lin).astype(BF16), wdn_ref[...])

    @pl.when(c == pl.num_programs(0) - 1)
    def _():
        y_ref[...] = x_ref[...] + _rms(acc_scr[...], g_ref[...])


def _ffn_cs(d_ff):
    for cs in (512, 256, 128):
        if d_ff % cs == 0:
            return cs
    raise ValueError(f"d_ff={d_ff} is not a multiple of the 128-lane tile")


def _ffn_prompt(gated_a, gated_b, x, w_out, g_post, g_pre, w_up, conv_w, conv_b, w_down, gain, batch, seq, tm):
    n, d = x.shape
    d_ff = w_down.shape[0]
    n_t = seq // tm
    full = lambda a: pl.BlockSpec(a.shape, lambda b, t: (0, 0), pipeline_mode=pl.Buffered(1))
    row = pl.BlockSpec((tm, d), lambda b, t: (b * n_t + t, 0))
    cs = _ffn_cs(d_ff)
    halo = 8
    return pl.pallas_call(
        functools.partial(_ffn_prompt_kernel, d_ff=d_ff, cs=cs),
        grid=(batch, n_t),
        in_specs=[row, row, row, full(w_out), full(g_post), full(g_pre),
                  full(w_up), full(conv_w), full(conv_b), full(w_down), full(gain)],
        out_specs=[row, pl.BlockSpec((1, CONV_W - 1, 2 * d_ff), lambda b, t: (b, 0, 0))],
        out_shape=[jax.ShapeDtypeStruct((n, d), F32),
                   jax.ShapeDtypeStruct((batch, CONV_W - 1, 2 * d_ff), F32)],
        scratch_shapes=[pltpu.VMEM((halo, 2 * d_ff), F32), pltpu.VMEM((4, halo + tm, cs), F32),
                        pltpu.VMEM((tm, d_ff), BF16)],
        compiler_params=_cparams(("parallel", "arbitrary")),
        name="ffn_prompt",
    )(gated_a, gated_b, x, w_out, g_post, g_pre, w_up, conv_w, conv_b, w_down, gain)


def _ffn_sample(h2, x1, w_up, conv_w, conv_b, w_down, gain, state, seq):
    n, d = x1.shape
    d_ff = w_down.shape[0]
    batch = n // seq
    cs = _ffn_cs(d_ff)
    nc = d_ff // cs
    full = lambda a: pl.BlockSpec(a.shape, lambda c: (0, 0))
    gate = lambda rows: pl.BlockSpec((rows, cs), lambda c: (0, c))
    lin = lambda rows: pl.BlockSpec((rows, cs), lambda c: (0, nc + c))
    st_gate = pl.BlockSpec((batch, CONV_W - 1, cs), lambda c: (0, 0, c))
    st_lin = pl.BlockSpec((batch, CONV_W - 1, cs), lambda c: (0, 0, nc + c))
    half_state = jax.ShapeDtypeStruct((batch, CONV_W - 1, d_ff), F32)
    y, new_gate, new_lin = pl.pallas_call(
        _ffn_sample_kernel,
        grid=(nc,),
        in_specs=[full(h2), full(x1), gate(d), lin(d), gate(CONV_W), lin(CONV_W), gate(1), lin(1),
                  pl.BlockSpec((cs, d), lambda c: (c, 0)), full(gain), st_gate, st_lin],
        out_specs=[full(x1), st_gate, st_gate],
        out_shape=[jax.ShapeDtypeStruct((n, d), F32), half_state, half_state],
        scratch_shapes=[pltpu.VMEM((n, d), F32)],
        compiler_params=_cparams(("arbitrary",)),
        name="ffn_sample",
    )(h2, x1, w_up, w_up, conv_w, conv_w, conv_b, conv_b, w_down, gain, state, state)
    return y, jnp.concatenate([new_gate, new_lin], axis=-1)


def _row_tile(n, cap):
    t = min(n, cap)
    while n % t:
        t //= 2
    return t


def kernel(x_prompt, x_sample, cache_k, cache_v, page_table, state_hgrn, state_ffn_conv, norm_mix_pre, norm_mix_post, w_in, hgrn_lb_logits, hgrn_norm, w_out, norm_ffn_pre, norm_ffn_post, w_up, conv_w, conv_b, w_down):
    bp, tp, d = x_prompt.shape
    bs, ts, _ = x_sample.shape
    depth = w_in.shape[0]
    heads = d // HEAD_DIM
    n_pool, page = cache_k.shape[1], cache_k.shape[2]
    lb_all = jnp.cumsum(jax.nn.softmax(hgrn_lb_logits.astype(F32), axis=0), axis=0)

    xp = x_prompt.reshape(bp * tp, d)
    xs = x_sample.reshape(bs * ts, d)
    outs = [[] for _ in range(8)]
    for l in range(depth):
        wl = w_in[l]
        w_perm = jnp.concatenate([wl[:, :5 * d], wl[:, 7 * d:], wl[:, 5 * d:7 * d]], axis=1).astype(BF16)
        w_out_b = w_out[l].astype(BF16)
        w_up_b = w_up[l].astype(BF16)
        w_down_b = w_down[l].astype(BF16)
        vec = lambda a: a[l].reshape(1, -1)
        lb = lb_all[l].reshape(1, d)
        Q_BLK, GA_BLK, GB_BLK = 4, 5, 6

        def mixers(x, batch, seq, s0, moba):
            proj, k_new, v_new = _inproj(x, vec(norm_mix_pre), w_perm, _row_tile(x.shape[0], 1024))
            gated_a, s_new = _hgrn(proj, lb, vec(hgrn_norm), s0, batch, seq, _row_tile(seq, 256), GA_BLK)
            return gated_a, moba(proj, k_new, v_new), k_new, v_new, s_new

        s0p = jnp.zeros((bp, heads, HEAD_DIM, HEAD_DIM), F32)
        ga, gb, kp, vp, sp = mixers(xp, bp, tp, s0p,
                                    lambda pr, k, v: _moba_prompt(pr, k, v, bp, tp, Q_BLK, GB_BLK))
        xp, cp = _ffn_prompt(ga, gb, xp, w_out_b, vec(norm_mix_post), vec(norm_ffn_pre), w_up_b, conv_w[l],
                             conv_b[l].reshape(1, -1), w_down_b, vec(norm_ffn_post), bp, tp, _row_tile(tp, 512))
        ga, gb, kn, vn, sn = mixers(xs, bs, ts, state_hgrn[l],
                                    lambda pr, k, v: _moba_sample(pr, k, v, cache_k, cache_v, page_table, l, ts,
                                                                  Q_BLK, GB_BLK))
        x1, h2 = _merge(ga, gb, xs, w_out_b, vec(norm_mix_post), vec(norm_ffn_pre), _row_tile(bs * ts, 512))
        xs, cn = _ffn_sample(h2, x1, w_up_b, conv_w[l], conv_b[l].reshape(1, -1), w_down_b,
                             vec(norm_ffn_post), state_ffn_conv[l], ts)
        kv = lambda a, b, t: a.reshape(b, t, heads, HEAD_DIM)
        for o, val in zip(outs, (kv(kp, bp, tp), kv(vp, bp, tp), sp, cp, kv(kn, bs, ts), kv(vn, bs, ts), sn, cn)):
            o.append(val)
    return (xp.reshape(bp, tp, d), xs.reshape(bs, ts, d), *(jnp.stack(o) for o in outs))
```

```python
import functools
import math

import jax
import jax.numpy as jnp
from jax import lax
from jax.experimental import pallas as pl
from jax.experimental.pallas import tpu as pltpu

F32 = jnp.float32
BF16 = jnp.bfloat16

NORM_EPS = 1e-6
HEAD_DIM = 128
MOBA_BLOCK = 256
MOBA_TOPK = 3
HGRN_CHUNK = 64
CONV_W = 3
BF16_ROWS = 16
SOFTMAX_LOG2_SCALE = math.log2(math.e) / math.sqrt(HEAD_DIM)
VMEM_LIMIT = 56 * 1024 * 1024


def _cparams(sem):
    return pltpu.CompilerParams(dimension_semantics=sem, vmem_limit_bytes=VMEM_LIMIT)


def _sigmoid(x):
    return 1.0 / (1.0 + jnp.exp(-x))


def _rms(x, gain):
    return x * lax.rsqrt(jnp.mean(x * x, axis=-1, keepdims=True) + NORM_EPS) * gain


def _dot(a, b):
    return jnp.dot(a, b, preferred_element_type=F32)


def _dot_nt(a, b):
    return lax.dot_general(a, b, (((1,), (1,)), ((), ())), preferred_element_type=F32)


def _dot_tn(a, b):
    return lax.dot_general(a, b, (((0,), (0,)), ((), ())), preferred_element_type=F32)


def _split_bf16(x, parts):
    out = []
    for _ in range(parts):
        p = x.astype(BF16)
        out.append(p)
        x = x - p.astype(F32)
    return out


def _dot_nt_split(a, b, operand_dtype=BF16):
    ah, al = (p.astype(operand_dtype) for p in _split_bf16(a, 2))
    bh, bl = (p.astype(operand_dtype) for p in _split_bf16(b, 2))
    return _dot_nt(ah, bh) + _dot_nt(ah, bl) + _dot_nt(al, bh)


def _inproj_kernel(x_ref, g_ref, w_ref, p_ref, k_ref, v_ref, h_scr, *, main_steps, kv_steps):
    j = pl.program_id(1)

    @pl.when(j == 0)
    def _():
        h_scr[...] = _rms(x_ref[...], g_ref[...]).astype(BF16)

    @pl.when(j < main_steps)
    def _():
        p_ref[...] = _dot(h_scr[...], w_ref[...])

    @pl.when((j >= main_steps) & (j < main_steps + kv_steps))
    def _():
        k_ref[...] = _dot(h_scr[...], w_ref[...])

    @pl.when(j >= main_steps + kv_steps)
    def _():
        v_ref[...] = _dot(h_scr[...], w_ref[...])


def _inproj(x, gain, w_in, tm, tn):
    n, d = x.shape
    per = d // tn
    main_steps, kv_steps = 7 * per, per
    src_block = lambda jb: jnp.where(jb < 5, jb, jnp.where(jb < 7, jb + 2, jb - 2))
    return pl.pallas_call(
        functools.partial(_inproj_kernel, main_steps=main_steps, kv_steps=kv_steps),
        grid=(n // tm, 9 * per),
        in_specs=[
            pl.BlockSpec((tm, d), lambda i, j: (i, 0)),
            pl.BlockSpec((1, d), lambda i, j: (0, 0)),
            pl.BlockSpec((d, tn), lambda i, j: (0, src_block(j // per) * per + j % per)),
        ],
        out_specs=[
            pl.BlockSpec((tm, tn), lambda i, j: (i, jnp.minimum(j, main_steps - 1))),
            pl.BlockSpec((tm, tn), lambda i, j: (i, jnp.clip(j - main_steps, 0, kv_steps - 1))),
            pl.BlockSpec((tm, tn), lambda i, j: (i, jnp.clip(j - main_steps - kv_steps, 0, kv_steps - 1))),
        ],
        out_shape=[
            jax.ShapeDtypeStruct((n, 7 * d), F32),
            jax.ShapeDtypeStruct((n, d), F32),
            jax.ShapeDtypeStruct((n, d), F32),
        ],
        scratch_shapes=[pltpu.VMEM((tm, d), BF16)],
        compiler_params=_cparams(("parallel", "arbitrary")),
        name="inproj",
    )(x, gain, w_in)


def _hgrn_kernel(hq_ref, hf_ref, hi_ref, hg_ref, ga_ref, lb_ref, gain_ref, s0_ref, a_ref, sout_ref, st_scr,
                 *, heads):
    tt = pl.program_id(1)
    C = HGRN_CHUNK
    assert 2 * C == HEAD_DIM
    rows = hq_ref.shape[0]
    d = heads * HEAD_DIM
    pad = (-rows) % C
    t = rows + pad

    @pl.when(tt == 0)
    def _():
        for h in range(heads):
            st_scr[h] = s0_ref[0, h].T

    def padded(x, fill):
        return jnp.concatenate([x, jnp.full((pad, d), fill, F32)], axis=0) if pad else x

    lb = lb_ref[...]
    f = padded(lb + (1.0 - lb) * _sigmoid(hf_ref[...]), 1.0)
    kk = 1.0 - f
    hq = hq_ref[...]
    q = padded(hq * _sigmoid(hq), 0.0)
    v = padded(hi_ref[...], 0.0)
    hg = hg_ref[...]
    gate = hg * _sigmoid(hg)
    gain = gain_ref[...]
    tril = jnp.where(lax.broadcasted_iota(jnp.int32, (t, t), 0) >= lax.broadcasted_iota(jnp.int32, (t, t), 1),
                     1.0, 0.0).astype(BF16)
    bcum = functools.reduce(jnp.add, [_dot(tril, part) for part in _split_bf16(jnp.log(f), 3)])
    causal = (lax.broadcasted_iota(jnp.int32, (C, HEAD_DIM), 1) <= lax.broadcasted_iota(jnp.int32, (C, HEAD_DIM), 0))
    zero_rows = jnp.zeros((C, HEAD_DIM), BF16)
    states = [st_scr[h] for h in range(heads)]
    tiles = []

    for c in range(t // C):
        rs = slice(c * C, (c + 1) * C)
        b = bcum[rs] if c == 0 else bcum[rs] - bcum[c * C - 1:c * C]
        b_last = b[C - 1:C]
        qt = (q[rs] * jnp.exp(b)).astype(BF16)
        kt = (kk[rs] * jnp.exp(-b)).astype(BF16)
        kh = (kk[rs] * jnp.exp(b_last - b)).astype(BF16)
        decay = jnp.exp(b_last)
        vc = v[rs]
        vb = vc.astype(BF16)
        sls = [slice(h * HEAD_DIM, (h + 1) * HEAD_DIM) for h in range(heads)]
        out1 = [_dot_nt(qt[:, sl], jnp.concatenate([st.astype(BF16), kt[:, sl], zero_rows], axis=0))
                for sl, st in zip(sls, states)]
        v_t = [jnp.concatenate([jnp.zeros((C, HEAD_DIM), F32), vc[:, sl]], axis=0).T.astype(BF16) for sl in sls]
        scores = [jnp.where(causal, o1[:, HEAD_DIM:], 0.0).astype(BF16) for o1 in out1]
        out2 = [_dot(jnp.concatenate([sc, vt], axis=0), jnp.concatenate([vb[:, sl], kh[:, sl]], axis=0))
                for sc, vt, sl in zip(scores, v_t, sls)]
        states = [st * decay[:, sl] + o2[C:] for st, o2, sl in zip(states, out2, sls)]
        tiles.append(jnp.concatenate([_rms(o1[:, :HEAD_DIM] + o2[:C], gain[:, sl])
                                      for o1, o2, sl in zip(out1, out2, sls)], axis=1))

    a_ref[...] = _sigmoid(ga_ref[...]) * (jnp.concatenate(tiles, axis=0)[:rows] * gate)
    for h in range(heads):
        st_scr[h] = states[h]

    @pl.when(tt == pl.num_programs(1) - 1)
    def _():
        for h in range(heads):
            sout_ref[0, h] = st_scr[h].T


def _hgrn(proj, lb, gain, s0, batch, seq, t_tile, ga_block):
    n = proj.shape[0]
    d = lb.shape[1]
    heads = d // HEAD_DIM
    n_t = seq // t_tile
    col_spec = lambda cb: pl.BlockSpec((t_tile, d), lambda b, t: (b * n_t + t, cb))
    state_spec = pl.BlockSpec((1, heads, HEAD_DIM, HEAD_DIM), lambda b, t: (b, 0, 0, 0))
    return pl.pallas_call(
        functools.partial(_hgrn_kernel, heads=heads),
        grid=(batch, n_t),
        in_specs=[col_spec(0), col_spec(1), col_spec(2), col_spec(3), col_spec(ga_block),
                  pl.BlockSpec((1, d), lambda b, t: (0, 0)),
                  pl.BlockSpec((1, d), lambda b, t: (0, 0)),
                  state_spec],
        out_specs=[pl.BlockSpec((t_tile, d), lambda b, t: (b * n_t + t, 0)), state_spec],
        out_shape=[jax.ShapeDtypeStruct((n, d), F32),
                   jax.ShapeDtypeStruct((batch, heads, HEAD_DIM, HEAD_DIM), F32)],
        scratch_shapes=[pltpu.VMEM((heads, HEAD_DIM, HEAD_DIM), F32)],
        compiler_params=_cparams(("parallel", "arbitrary")),
        name="hgrn",
    )(proj, proj, proj, proj, proj, lb, gain, s0)


def _topk_bias(s, valid, topk):
    n_blocks = len(valid)
    rows = []
    for n in range(n_blocks):
        sn = s[n:n + 1, :]
        cnt = jnp.zeros(sn.shape, jnp.int32)
        for m in range(n_blocks):
            if m == n:
                continue
            sm = s[m:m + 1, :]
            beats = (sm >= sn) if m < n else (sm > sn)
            cnt = cnt + jnp.where(beats & valid[m], 1, 0)
        rows.append(jnp.where(cnt < topk, 0.0, -jnp.inf))
    return rows


def _moba_prompt_kernel(q_ref, k_ref, v_ref, gb_ref, o_ref, *, nb, blk):
    t = nb * blk
    q = q_ref[...]
    k = k_ref[...]
    qb = (q * SOFTMAX_LOG2_SCALE).astype(BF16)
    kb = k.astype(BF16)
    vt = v_ref[...].T.astype(BF16)

    kmean = jnp.concatenate(
        [jnp.mean(k[n * blk:(n + 1) * blk, :], axis=0, keepdims=True) for n in range(nb)]
        + [jnp.zeros((BF16_ROWS - nb % BF16_ROWS, HEAD_DIM), F32)], axis=0)
    s = _dot_nt_split(kmean, q)
    q_blk = lax.broadcasted_iota(jnp.int32, (1, t), 1) // blk
    bias = _topk_bias(s, [q_blk > m for m in range(nb - 1)], MOBA_TOPK)

    key = lax.broadcasted_iota(jnp.int32, (blk, blk), 0)
    qry = lax.broadcasted_iota(jnp.int32, (blk, blk), 1)
    logits = lambda j: _dot_nt(kb[:(j + 1) * blk, :], qb[j * blk:(j + 1) * blk, :])
    lg_next = logits(0)
    for j in range(nb):
        qs = slice(j * blk, (j + 1) * blk)
        lg = lg_next
        if j + 1 < nb:
            lg_next = logits(j + 1)
        parts = [lg[n * blk:(n + 1) * blk, :] + bias[n][:, qs] for n in range(j)]
        parts.append(jnp.where(key <= qry, lg[j * blk:, :], -jnp.inf))
        m = functools.reduce(jnp.maximum, [jnp.max(p, axis=0, keepdims=True) for p in parts])
        ps = [jnp.exp2(p - m) for p in parts]
        l = functools.reduce(jnp.add, [jnp.sum(p, axis=0, keepdims=True) for p in ps])
        pb = jnp.concatenate([p.astype(BF16) for p in ps], axis=0)
        o_ref[qs, :] = _sigmoid(gb_ref[qs, :]) * (_dot(vt[:, :(j + 1) * blk], pb) / l).T


def _moba_prompt(proj, k_new, v_new, batch, seq, q_col_block, gb_col_block):
    n, d = k_new.shape
    heads = d // HEAD_DIM
    blk = MOBA_BLOCK
    nb = seq // blk
    seq_head = lambda cb: pl.BlockSpec((seq, HEAD_DIM), lambda b, h: (b, cb * heads + h))
    return pl.pallas_call(
        functools.partial(_moba_prompt_kernel, nb=nb, blk=blk),
        grid=(batch, heads),
        in_specs=[seq_head(q_col_block), seq_head(0), seq_head(0), seq_head(gb_col_block)],
        out_specs=seq_head(0),
        out_shape=jax.ShapeDtypeStruct((n, d), F32),
        compiler_params=_cparams(("parallel", "parallel")),
        name="moba_prompt",
    )(proj, k_new, v_new, proj)


def _block_diag_queries(q, heads, rows):
    d = q.shape[1]
    tiled = jnp.concatenate([q] * heads, axis=0)
    r = lax.broadcasted_iota(jnp.int32, (heads * rows, d), 0) // rows
    c = lax.broadcasted_iota(jnp.int32, (heads * rows, d), 1) // HEAD_DIM
    return jnp.where(r == c, tiled, 0.0)


def _moba_sample_kernel(pt_ref, q_ref, kn_ref, vn_ref, gb_ref, *refs, heads, rows, page, pages_per_step):
    del pt_ref
    k_refs, v_refs = refs[:pages_per_step], refs[pages_per_step:2 * pages_per_step]
    o_ref, lg_scr, ksum_scr, kmean_scr, bias_scr, hmask_scr, m_scr, l_scr, acc_scr = refs[2 * pages_per_step:]
    phase, g = pl.program_id(1), pl.program_id(2)
    blocks = range(pages_per_step // 2)
    hr = heads * rows
    d = heads * HEAD_DIM
    nb = bias_scr.shape[0]
    head = lambda h: slice(h * HEAD_DIM, (h + 1) * HEAD_DIM)
    pos_head = lambda ref: ref[...].reshape(page * heads, HEAD_DIM).astype(BF16)
    side_by_side = lambda refs, i: jnp.concatenate([pos_head(refs[2 * i]), pos_head(refs[2 * i + 1])], axis=1)

    q = q_ref[...]
    qs = q * SOFTMAX_LOG2_SCALE

    @pl.when(phase == 0)
    def _():
        for i, k_ref in enumerate(k_refs):
            ksum_scr[g * pages_per_step + i] = jnp.sum(k_ref[...], axis=0)
        q_hd = jnp.concatenate([qs[:, head(h)] for h in range(heads)], axis=0).astype(BF16)
        zero = jnp.zeros_like(q_hd)
        q_pair = jnp.concatenate([jnp.concatenate([q_hd, zero], axis=1),
                                  jnp.concatenate([zero, q_hd], axis=1)], axis=0)
        for i in blocks:
            lg_scr[g * len(blocks) + i] = _dot_nt(q_pair, side_by_side(k_refs, i))

    @pl.when((phase == 1) & (g == 0))
    def _():
        for n in range(nb):
            kmean_scr[n * heads:(n + 1) * heads, :] = (ksum_scr[2 * n] + ksum_scr[2 * n + 1]) * (0.5 / page)
        s = jnp.concatenate([_dot_nt_split(q[:, head(h)], kmean_scr[pl.ds(h, nb, stride=heads), :], F32)
                             for h in range(heads)], axis=0)
        cnt = jnp.zeros((hr, nb), jnp.int32)
        idx = lax.broadcasted_iota(jnp.int32, (hr, nb), 1)
        for m in range(nb):
            sm = s[:, m:m + 1]
            beats = (sm > s) | ((sm == s) & (idx > m))
            cnt = cnt + jnp.where(beats, 1, 0)
        bias = jnp.where(cnt < MOBA_TOPK, 0.0, -jnp.inf)
        for n in range(nb):
            col = jnp.broadcast_to(bias[:, n:n + 1], (hr, HEAD_DIM))
            bias_scr[n] = jnp.concatenate([col, col], axis=0)
        r = lax.broadcasted_iota(jnp.int32, hmask_scr.shape, 0)
        c = lax.broadcasted_iota(jnp.int32, hmask_scr.shape, 1)
        hmask_scr[...] = jnp.where(c % heads == (r % hr) // rows, 0.0, -jnp.inf)

        qbd = _block_diag_queries(qs, heads, rows).astype(BF16)
        kn = jnp.concatenate([kn_ref[...], jnp.zeros((page - rows, d), F32)], axis=0)
        vn = jnp.concatenate([vn_ref[...], jnp.zeros((page - rows, d), F32)], axis=0)
        r_own = lax.broadcasted_iota(jnp.int32, (hr, page), 0) % rows
        c_own = lax.broadcasted_iota(jnp.int32, (hr, page), 1)
        lg_own = jnp.where(c_own <= r_own, _dot_nt(qbd, kn.astype(BF16)), -jnp.inf)
        m0 = jnp.max(lg_own, axis=1, keepdims=True)
        p_own = jnp.exp2(lg_own - m0)
        pv = _dot(p_own.astype(BF16), vn.astype(BF16))
        m_scr[...] = m0
        l_scr[...] = jnp.sum(p_own, axis=1, keepdims=True)
        acc_scr[...] = jnp.concatenate([pv[h * rows:(h + 1) * rows, head(h)] for h in range(heads)], axis=0)

    @pl.when(phase == 1)
    def _():
        hmask = hmask_scr[...]
        lgs = [lg_scr[g * len(blocks) + i] + hmask
               + jnp.concatenate([bias_scr[g * len(blocks) + i]] * heads, axis=1) for i in blocks]
        m_row = functools.reduce(jnp.maximum, [jnp.max(lg, axis=1, keepdims=True) for lg in lgs])
        m_old = m_scr[...]
        m_new = jnp.maximum(m_old, jnp.maximum(m_row[:hr], m_row[hr:]))
        alpha = jnp.exp2(m_old - m_new)
        m_pair = jnp.concatenate([m_new, m_new], axis=0)
        ps = [jnp.exp2(lg - m_pair) for lg in lgs]
        l_row = functools.reduce(jnp.add, [jnp.sum(p, axis=1, keepdims=True) for p in ps])
        pv = _dot(jnp.concatenate([p.astype(BF16) for p in ps], axis=1),
                  jnp.concatenate([side_by_side(v_refs, i) for i in blocks], axis=0))
        m_scr[...] = m_new
        l_scr[...] = alpha * l_scr[...] + l_row[:hr] + l_row[hr:]
        acc_scr[...] = alpha * acc_scr[...] + pv[:hr, :HEAD_DIM] + pv[hr:, HEAD_DIM:]

    @pl.when((phase == 1) & (g == pl.num_programs(2) - 1))
    def _():
        out = acc_scr[...] / l_scr[...]
        for h in range(heads):
            o_ref[:, head(h)] = _sigmoid(gb_ref[:, head(h)]) * out[h * rows:(h + 1) * rows, :]


def _moba_sample(proj, k_new, v_new, cache_k, cache_v, page_table, layer, rows, q_col_block, gb_col_block):
    n, d = k_new.shape
    _, _, page, heads, dh = cache_k.shape
    batch, n_pages = page_table.shape
    assert dh == HEAD_DIM and 2 * page == MOBA_BLOCK and n_pages % 2 == 0
    nb = n_pages // 2
    hr = heads * rows
    pt = page_table.reshape(-1)

    pps = 2 * math.gcd(nb, 4)
    steps = n_pages // pps

    def page_spec(step_of, i):
        return pl.BlockSpec((None, None, page, heads, dh),
                            lambda b, ph, g, pt: (layer, pt[b * n_pages + step_of(ph, g) * pps + i], 0, 0, 0))

    k_step = lambda ph, g: jnp.where(ph == 0, g, steps - 1)
    v_step = lambda ph, g: jnp.where(ph == 0, 0, g)
    row_spec = lambda cb: pl.BlockSpec((rows, d), lambda b, ph, g, pt: (b, cb))
    return pl.pallas_call(
        functools.partial(_moba_sample_kernel, heads=heads, rows=rows, page=page, pages_per_step=pps),
        grid_spec=pltpu.PrefetchScalarGridSpec(
            num_scalar_prefetch=1,
            grid=(batch, 2, steps),
            in_specs=[row_spec(q_col_block), row_spec(0), row_spec(0), row_spec(gb_col_block)]
                     + [page_spec(k_step, i) for i in range(pps)] + [page_spec(v_step, i) for i in range(pps)],
            out_specs=row_spec(0),
            scratch_shapes=[pltpu.VMEM((nb, 2 * hr, page * heads), F32),
                            pltpu.VMEM((n_pages, heads, dh), F32),
                            pltpu.VMEM((nb * heads, dh), F32),
                            pltpu.VMEM((nb, 2 * hr, dh), F32),
                            pltpu.VMEM((2 * hr, page * heads), F32),
                            pltpu.VMEM((hr, 1), F32), pltpu.VMEM((hr, 1), F32), pltpu.VMEM((hr, dh), F32)],
        ),
        out_shape=jax.ShapeDtypeStruct((n, d), F32),
        compiler_params=_cparams(("parallel", "arbitrary", "arbitrary")),
        name="moba_sample",
    )(pt, proj, k_new, v_new, proj, *([cache_k] * pps), *([cache_v] * pps))


def _merge_rows(gated_a, gated_b, x, w_out, g_post, g_pre):
    m = _dot((gated_a + gated_b).astype(BF16), w_out)
    x1 = x + _rms(m, g_post)
    return x1, _rms(x1, g_pre).astype(BF16)


def _merge_kernel(a_ref, b_ref, x_ref, w_ref, gpost_ref, gpre_ref, x1_ref, h2_ref):
    x1_ref[...], h2_ref[...] = _merge_rows(a_ref[...], b_ref[...], x_ref[...], w_ref[...],
                                           gpost_ref[...], gpre_ref[...])


def _merge(gated_a, gated_b, x, w_out, g_post, g_pre, tm):
    n, d = x.shape
    row = pl.BlockSpec((tm, d), lambda i: (i, 0))
    vec = pl.BlockSpec((1, d), lambda i: (0, 0))
    return pl.pallas_call(
        _merge_kernel,
        grid=(n // tm,),
        in_specs=[row, row, row, pl.BlockSpec((d, d), lambda i: (0, 0)), vec, vec],
        out_specs=[row, row],
        out_shape=[jax.ShapeDtypeStruct((n, d), F32), jax.ShapeDtypeStruct((n, d), BF16)],
        compiler_params=_cparams(("parallel",)),
        name="merge",
    )(gated_a, gated_b, x, w_out, g_post, g_pre)


def _gelu_tanh(x):
    k = -2.0 * math.sqrt(2.0 / math.pi) * math.log2(math.e)
    return x / (1.0 + jnp.exp2(x * (k + (k * 0.044715) * (x * x))))


def _ffn_prompt_kernel(a_ref, b_ref, x_ref, wout_ref, gpost_ref, gpre_ref, wup_ref, cw_ref, cb_ref, wdn_ref,
                       g_ref, y_ref, st_ref, carry_scr, win_scr, act_scr, *, d_ff, cs):
    tm = x_ref.shape[0]
    x1, h = _merge_rows(a_ref[...], b_ref[...], x_ref[...], wout_ref[...], gpost_ref[...], gpre_ref[...])
    halo = carry_scr.shape[0]

    @pl.when(pl.program_id(1) == 0)
    def _():
        carry_scr[...] = jnp.zeros_like(carry_scr)

    def conv(col0, win):
        cols = slice(col0, col0 + cs)
        up = _dot(h, wup_ref[:, cols])
        win[0:halo, :] = carry_scr[:, cols]
        win[halo:, :] = up
        carry_scr[:, cols] = up[tm - halo:, :]
        st_ref[0, :, cols] = up[tm - (CONV_W - 1):, :]
        cw = cw_ref[:, cols]
        return (cb_ref[:, cols] + cw[0:1, :] * win[halo - 2:halo - 2 + tm, :]
                + cw[1:2, :] * win[halo - 1:halo - 1 + tm, :] + cw[2:3, :] * up)

    for c in range(d_ff // cs):
        gate = conv(c * cs, win_scr.at[(2 * c) % win_scr.shape[0]])
        lin = conv(d_ff + c * cs, win_scr.at[(2 * c + 1) % win_scr.shape[0]])
        act_scr[:, c * cs:(c + 1) * cs] = (_gelu_tanh(gate) * lin).astype(BF16)
    y_ref[...] = x1 + _rms(_dot(act_scr[...], wdn_ref[...]), g_ref[...])


def _ffn_sample_kernel(h_ref, x_ref, wg_ref, wl_ref, cwg_ref, cwl_ref, cbg_ref, cbl_ref, wdn_ref, g_ref,
                       sg_ref, sl_ref, y_ref, ng_ref, nl_ref, acc_scr):
    c = pl.program_id(0)
    h = h_ref[...]
    batch, _, cs = sg_ref.shape
    seq_rows = h.shape[0] // batch
    t = lax.broadcasted_iota(jnp.int32, (batch, seq_rows, cs), 1)

    @pl.when(c == 0)
    def _():
        acc_scr[...] = jnp.zeros_like(acc_scr)

    def conv(w_ref, s_ref, n_ref, cw_ref, cb_ref):
        up = _dot(h, w_ref[...])
        by_seq = lambda a: a.reshape(batch, seq_rows, cs)
        n_ref[...] = by_seq(up)[:, seq_rows - (CONV_W - 1):, :]
        prev2, prev1 = s_ref[:, 0:1, :], s_ref[:, 1:2, :]
        p1 = jnp.where(t == 0, prev1, by_seq(pltpu.roll(up, 1, 0)))
        p2 = jnp.where(t == 0, prev2, jnp.where(t == 1, prev1, by_seq(pltpu.roll(up, 2, 0))))
        cw = cw_ref[...]
        return (cb_ref[...] + cw[0:1, :] * p2 + cw[1:2, :] * p1 + cw[2:3, :] * by_seq(up)).reshape(up.shape)

    gate = conv(wg_ref, sg_ref, ng_ref, cwg_ref, cbg_ref)
    lin = conv(wl_ref, sl_ref, nl_ref, cwl_ref, cbl_ref)
    acc_scr[...] += _dot((_gelu_tanh(gate) * lin).astype(BF16), wdn_ref[...])

    @pl.when(c == pl.num_programs(0) - 1)
    def _():
        y_ref[...] = x_ref[...] + _rms(acc_scr[...], g_ref[...])


def _ffn_cs(d_ff):
    for cs in (512, 256, 128):
        if d_ff % cs == 0:
            return cs
    raise ValueError(f"d_ff={d_ff} is not a multiple of the 128-lane tile")


def _ffn_prompt(gated_a, gated_b, x, w_out, g_post, g_pre, w_up, conv_w, conv_b, w_down, gain, batch, seq, tm):
    n, d = x.shape
    d_ff = w_down.shape[0]
    n_t = seq // tm
    full = lambda a: pl.BlockSpec(a.shape, lambda b, t: (0, 0), pipeline_mode=pl.Buffered(1))
    row = pl.BlockSpec((tm, d), lambda b, t: (b * n_t + t, 0))
    cs = _ffn_cs(d_ff)
    halo = 8
    return pl.pallas_call(
        functools.partial(_ffn_prompt_kernel, d_ff=d_ff, cs=cs),
        grid=(batch, n_t),
        in_specs=[row, row, row, full(w_out), full(g_post), full(g_pre),
                  full(w_up), full(conv_w), full(conv_b), full(w_down), full(gain)],
        out_specs=[row, pl.BlockSpec((1, CONV_W - 1, 2 * d_ff), lambda b, t: (b, 0, 0))],
        out_shape=[jax.ShapeDtypeStruct((n, d), F32),
                   jax.ShapeDtypeStruct((batch, CONV_W - 1, 2 * d_ff), F32)],
        scratch_shapes=[pltpu.VMEM((halo, 2 * d_ff), F32), pltpu.VMEM((4, halo + tm, cs), F32),
                        pltpu.VMEM((tm, d_ff), BF16)],
        compiler_params=_cparams(("parallel", "arbitrary")),
        name="ffn_prompt",
    )(gated_a, gated_b, x, w_out, g_post, g_pre, w_up, conv_w, conv_b, w_down, gain)


def _ffn_sample(h2, x1, w_up, conv_w, conv_b, w_down, gain, state, seq):
    n, d = x1.shape
    d_ff = w_down.shape[0]
    batch = n // seq
    cs = _ffn_cs(d_ff)
    nc = d_ff // cs
    full = lambda a: pl.BlockSpec(a.shape, lambda c: (0, 0))
    gate = lambda rows: pl.BlockSpec((rows, cs), lambda c: (0, c))
    lin = lambda rows: pl.BlockSpec((rows, cs), lambda c: (0, nc + c))
    st_gate = pl.BlockSpec((batch, CONV_W - 1, cs), lambda c: (0, 0, c))
    st_lin = pl.BlockSpec((batch, CONV_W - 1, cs), lambda c: (0, 0, nc + c))
    half_state = jax.ShapeDtypeStruct((batch, CONV_W - 1, d_ff), F32)
    y, new_gate, new_lin = pl.pallas_call(
        _ffn_sample_kernel,
        grid=(nc,),
        in_specs=[full(h2), full(x1), gate(d), lin(d), gate(CONV_W), lin(CONV_W), gate(1), lin(1),
                  pl.BlockSpec((cs, d), lambda c: (c, 0)), full(gain), st_gate, st_lin],
        out_specs=[full(x1), st_gate, st_gate],
        out_shape=[jax.ShapeDtypeStruct((n, d), F32), half_state, half_state],
        scratch_shapes=[pltpu.VMEM((n, d), F32)],
        compiler_params=_cparams(("arbitrary",)),
        name="ffn_sample",
    )(h2, x1, w_up, w_up, conv_w, conv_w, conv_b, conv_b, w_down, gain, state, state)
    return y, jnp.concatenate([new_gate, new_lin], axis=-1)


def _row_tile(n, cap):
    t = min(n, cap)
    while n % t:
        t //= 2
    return t


def kernel(x_prompt, x_sample, cache_k, cache_v, page_table, state_hgrn, state_ffn_conv, norm_mix_pre, norm_mix_post, w_in, hgrn_lb_logits, hgrn_norm, w_out, norm_ffn_pre, norm_ffn_post, w_up, conv_w, conv_b, w_down):
    bp, tp, d = x_prompt.shape
    bs, ts, _ = x_sample.shape
    depth = w_in.shape[0]
    heads = d // HEAD_DIM
    n_pool, page = cache_k.shape[1], cache_k.shape[2]
    lb_all = jnp.cumsum(jax.nn.softmax(hgrn_lb_logits.astype(F32), axis=0), axis=0)

    xp = x_prompt.reshape(bp * tp, d)
    xs = x_sample.reshape(bs * ts, d)
    outs = [[] for _ in range(8)]
    for l in range(depth):
        w_in_b = w_in[l].astype(BF16)
        w_out_b = w_out[l].astype(BF16)
        w_up_b = w_up[l].astype(BF16)
        w_down_b = w_down[l].astype(BF16)
        vec = lambda a: a[l].reshape(1, -1)
        lb = lb_all[l].reshape(1, d)
        Q_BLK, GA_BLK, GB_BLK = 4, 5, 6

        def mixers(x, batch, seq, s0, moba):
            proj, k_new, v_new = _inproj(x, vec(norm_mix_pre), w_in_b, _row_tile(x.shape[0], 2048), d // 2)
            gated_a, s_new = _hgrn(proj, lb, vec(hgrn_norm), s0, batch, seq, _row_tile(seq, 256), GA_BLK)
            return gated_a, moba(proj, k_new, v_new), k_new, v_new, s_new

        s0p = jnp.zeros((bp, heads, HEAD_DIM, HEAD_DIM), F32)
        ga, gb, kp, vp, sp = mixers(xp, bp, tp, s0p,
                                    lambda pr, k, v: _moba_prompt(pr, k, v, bp, tp, Q_BLK, GB_BLK))
        xp, cp = _ffn_prompt(ga, gb, xp, w_out_b, vec(norm_mix_post), vec(norm_ffn_pre), w_up_b, conv_w[l],
                             conv_b[l].reshape(1, -1), w_down_b, vec(norm_ffn_post), bp, tp, _row_tile(tp, 512))
        ga, gb, kn, vn, sn = mixers(xs, bs, ts, state_hgrn[l],
                                    lambda pr, k, v: _moba_sample(pr, k, v, cache_k, cache_v, page_table, l, ts,
                                                                  Q_BLK, GB_BLK))
        x1, h2 = _merge(ga, gb, xs, w_out_b, vec(norm_mix_post), vec(norm_ffn_pre), _row_tile(bs * ts, 512))
        xs, cn = _ffn_sample(h2, x1, w_up_b, conv_w[l], conv_b[l].reshape(1, -1), w_down_b,
                             vec(norm_ffn_post), state_ffn_conv[l], ts)
        kv = lambda a, b, t: a.reshape(b, t, heads, HEAD_DIM)
        for o, val in zip(outs, (kv(kp, bp, tp), kv(vp, bp, tp), sp, cp, kv(kn, bs, ts), kv(vn, bs, ts), sn, cn)):
            o.append(val)
    return (xp.reshape(bp, tp, d), xs.reshape(bs, ts, d), *(jnp.stack(o) for o in outs))
```

```python
import functools
import math

import jax
import jax.numpy as jnp
from jax import lax
from jax.experimental import pallas as pl
from jax.experimental.pallas import tpu as pltpu

F32 = jnp.float32
BF16 = jnp.bfloat16

NORM_EPS = 1e-6
HEAD_DIM = 128
MOBA_BLOCK = 256
MOBA_TOPK = 3
HGRN_CHUNK = 64
CONV_W = 3
BF16_ROWS = 16
SOFTMAX_LOG2_SCALE = math.log2(math.e) / math.sqrt(HEAD_DIM)
VMEM_LIMIT = 56 * 1024 * 1024


def _cparams(sem):
    return pltpu.CompilerParams(dimension_semantics=sem, vmem_limit_bytes=VMEM_LIMIT)


def _sigmoid(x):
    return 0.5 * jnp.tanh(0.5 * x) + 0.5


def _rms(x, gain):
    return x * lax.rsqrt(jnp.mean(x * x, axis=-1, keepdims=True) + NORM_EPS) * gain


def _dot(a, b):
    return jnp.dot(a, b, preferred_element_type=F32)


def _dot_nt(a, b):
    return lax.dot_general(a, b, (((1,), (1,)), ((), ())), preferred_element_type=F32)


def _dot_tn(a, b):
    return lax.dot_general(a, b, (((0,), (0,)), ((), ())), preferred_element_type=F32)


def _split_bf16(x, parts):
    out = []
    for _ in range(parts):
        p = x.astype(BF16)
        out.append(p)
        x = x - p.astype(F32)
    return out


def _dot_nt_split(a, b, operand_dtype=BF16):
    ah, al = (p.astype(operand_dtype) for p in _split_bf16(a, 2))
    bh, bl = (p.astype(operand_dtype) for p in _split_bf16(b, 2))
    return _dot_nt(ah, bh) + _dot_nt(ah, bl) + _dot_nt(al, bh)


def _inproj_kernel(x_ref, g_ref, w_ref, p_ref, k_ref, v_ref, h_scr, *, main_steps, kv_steps):
    j = pl.program_id(1)

    @pl.when(j == 0)
    def _():
        h_scr[...] = _rms(x_ref[...], g_ref[...]).astype(BF16)

    @pl.when(j < main_steps)
    def _():
        p_ref[...] = _dot(h_scr[...], w_ref[...])

    @pl.when((j >= main_steps) & (j < main_steps + kv_steps))
    def _():
        k_ref[...] = _dot(h_scr[...], w_ref[...])

    @pl.when(j >= main_steps + kv_steps)
    def _():
        v_ref[...] = _dot(h_scr[...], w_ref[...])


def _inproj(x, gain, w_in, tm, tn):
    n, d = x.shape
    per = d // tn
    main_steps, kv_steps = 7 * per, per
    src_block = lambda jb: jnp.where(jb < 5, jb, jnp.where(jb < 7, jb + 2, jb - 2))
    return pl.pallas_call(
        functools.partial(_inproj_kernel, main_steps=main_steps, kv_steps=kv_steps),
        grid=(n // tm, 9 * per),
        in_specs=[
            pl.BlockSpec((tm, d), lambda i, j: (i, 0)),
            pl.BlockSpec((1, d), lambda i, j: (0, 0)),
            pl.BlockSpec((d, tn), lambda i, j: (0, src_block(j // per) * per + j % per)),
        ],
        out_specs=[
            pl.BlockSpec((tm, tn), lambda i, j: (i, jnp.minimum(j, main_steps - 1))),
            pl.BlockSpec((tm, tn), lambda i, j: (i, jnp.clip(j - main_steps, 0, kv_steps - 1))),
            pl.BlockSpec((tm, tn), lambda i, j: (i, jnp.clip(j - main_steps - kv_steps, 0, kv_steps - 1))),
        ],
        out_shape=[
            jax.ShapeDtypeStruct((n, 7 * d), F32),
            jax.ShapeDtypeStruct((n, d), F32),
            jax.ShapeDtypeStruct((n, d), F32),
        ],
        scratch_shapes=[pltpu.VMEM((tm, d), BF16)],
        compiler_params=_cparams(("parallel", "arbitrary")),
        name="inproj",
    )(x, gain, w_in)


def _hgrn_kernel(hq_ref, hf_ref, hi_ref, hg_ref, ga_ref, lb_ref, gain_ref, s0_ref, a_ref, sout_ref, st_scr,
                 *, heads):
    tt = pl.program_id(1)
    C = HGRN_CHUNK
    assert 2 * C == HEAD_DIM
    rows = hq_ref.shape[0]
    d = heads * HEAD_DIM
    pad = (-rows) % C
    t = rows + pad

    @pl.when(tt == 0)
    def _():
        for h in range(heads):
            st_scr[h] = s0_ref[0, h].T

    lb = lb_ref[...]
    gain = gain_ref[...]
    tril = jnp.where(lax.broadcasted_iota(jnp.int32, (C, C), 0) >= lax.broadcasted_iota(jnp.int32, (C, C), 1),
                     1.0, 0.0).astype(BF16)
    causal = (lax.broadcasted_iota(jnp.int32, (C, HEAD_DIM), 1) <= lax.broadcasted_iota(jnp.int32, (C, HEAD_DIM), 0))
    zero_rows = jnp.zeros((C, HEAD_DIM), BF16)
    states = [st_scr[h] for h in range(heads)]

    def chunk_operands(c):
        live = min(rows, (c + 1) * C) - c * C
        rs = slice(c * C, c * C + live)

        def padded(x, fill):
            return jnp.concatenate([x, jnp.full((C - live, d), fill, F32)], axis=0) if live < C else x

        f = padded(lb + (1.0 - lb) * _sigmoid(hf_ref[rs, :]), 1.0)
        hq = hq_ref[rs, :]
        q = padded(hq * _sigmoid(hq), 0.0)
        vc = padded(hi_ref[rs, :], 0.0)
        b = functools.reduce(jnp.add, [_dot(tril, part) for part in _split_bf16(jnp.log(f), 3)])
        decay = jnp.exp(b[C - 1:C])
        k_scaled = (1.0 - f) * jnp.exp(-b)
        qt = (q * jnp.exp(b)).astype(BF16)
        kh = (k_scaled * decay).astype(BF16)
        return rs, live, decay, qt, k_scaled.astype(BF16), kh, vc, vc.astype(BF16)

    n_chunks = t // C
    ahead = chunk_operands(0)
    for c in range(n_chunks):
        rs, live, decay, qt, kt, kh, vc, vb = ahead
        if c + 1 < n_chunks:
            ahead = chunk_operands(c + 1)
        sls = [slice(h * HEAD_DIM, (h + 1) * HEAD_DIM) for h in range(heads)]
        out1 = [_dot_nt(qt[:, sl], jnp.concatenate([st.astype(BF16), kt[:, sl], zero_rows], axis=0))
                for sl, st in zip(sls, states)]
        v_t = [jnp.concatenate([jnp.zeros((C, HEAD_DIM), F32), vc[:, sl]], axis=0).T.astype(BF16) for sl in sls]
        scores = [jnp.where(causal, o1[:, HEAD_DIM:], 0.0).astype(BF16) for o1 in out1]
        out2 = [_dot(jnp.concatenate([sc, vt], axis=0), jnp.concatenate([vb[:, sl], kh[:, sl]], axis=0))
                for sc, vt, sl in zip(scores, v_t, sls)]
        states = [st * decay[:, sl] + o2[C:] for st, o2, sl in zip(states, out2, sls)]
        normed = jnp.concatenate([_rms(o1[:, :HEAD_DIM] + o2[:C], gain[:, sl])
                                  for o1, o2, sl in zip(out1, out2, sls)], axis=1)
        hg = hg_ref[rs, :]
        a_ref[rs, :] = _sigmoid(ga_ref[rs, :]) * (normed[:live] * (hg * _sigmoid(hg)))

    for h in range(heads):
        st_scr[h] = states[h]

    @pl.when(tt == pl.num_programs(1) - 1)
    def _():
        for h in range(heads):
            sout_ref[0, h] = st_scr[h].T


def _hgrn(proj, lb, gain, s0, batch, seq, t_tile, ga_block):
    n = proj.shape[0]
    d = lb.shape[1]
    heads = d // HEAD_DIM
    n_t = seq // t_tile
    col_spec = lambda cb: pl.BlockSpec((t_tile, d), lambda b, t: (b * n_t + t, cb))
    state_spec = pl.BlockSpec((1, heads, HEAD_DIM, HEAD_DIM), lambda b, t: (b, 0, 0, 0))
    return pl.pallas_call(
        functools.partial(_hgrn_kernel, heads=heads),
        grid=(batch, n_t),
        in_specs=[col_spec(0), col_spec(1), col_spec(2), col_spec(3), col_spec(ga_block),
                  pl.BlockSpec((1, d), lambda b, t: (0, 0)),
                  pl.BlockSpec((1, d), lambda b, t: (0, 0)),
                  state_spec],
        out_specs=[pl.BlockSpec((t_tile, d), lambda b, t: (b * n_t + t, 0)), state_spec],
        out_shape=[jax.ShapeDtypeStruct((n, d), F32),
                   jax.ShapeDtypeStruct((batch, heads, HEAD_DIM, HEAD_DIM), F32)],
        scratch_shapes=[pltpu.VMEM((heads, HEAD_DIM, HEAD_DIM), F32)],
        compiler_params=_cparams(("parallel", "arbitrary")),
        name="hgrn",
    )(proj, proj, proj, proj, proj, lb, gain, s0)


def _topk_bias(s, valid, topk):
    n_blocks = len(valid)
    rows = []
    for n in range(n_blocks):
        sn = s[n:n + 1, :]
        cnt = jnp.zeros(sn.shape, jnp.int32)
        for m in range(n_blocks):
            if m == n:
                continue
            sm = s[m:m + 1, :]
            beats = (sm >= sn) if m < n else (sm > sn)
            cnt = cnt + jnp.where(beats & valid[m], 1, 0)
        rows.append(jnp.where(cnt < topk, 0.0, -jnp.inf))
    return rows


def _moba_prompt_kernel(q_ref, k_ref, v_ref, gb_ref, o_ref, *, nb, blk):
    t = nb * blk
    q = q_ref[...]
    k = k_ref[...]
    qb = (q * SOFTMAX_LOG2_SCALE).astype(BF16)
    kb = k.astype(BF16)
    vt = v_ref[...].T.astype(BF16)

    kmean = jnp.concatenate(
        [jnp.mean(k[n * blk:(n + 1) * blk, :], axis=0, keepdims=True) for n in range(nb)]
        + [jnp.zeros((BF16_ROWS - nb % BF16_ROWS, HEAD_DIM), F32)], axis=0)
    s = _dot_nt_split(kmean, q)
    q_blk = lax.broadcasted_iota(jnp.int32, (1, t), 1) // blk
    bias = _topk_bias(s, [q_blk > m for m in range(nb - 1)], MOBA_TOPK)

    key = lax.broadcasted_iota(jnp.int32, (blk, blk), 0)
    qry = lax.broadcasted_iota(jnp.int32, (blk, blk), 1)
    logits = lambda j: _dot_nt(kb[:(j + 1) * blk, :], qb[j * blk:(j + 1) * blk, :])
    lg_next = logits(nb - 1)
    for j in reversed(range(nb)):
        qs = slice(j * blk, (j + 1) * blk)
        lg = lg_next
        if j > 0:
            lg_next = logits(j - 1)
        parts = [lg[n * blk:(n + 1) * blk, :] + bias[n][:, qs] for n in range(j)]
        parts.append(jnp.where(key <= qry, lg[j * blk:, :], -jnp.inf))
        m = functools.reduce(jnp.maximum, [jnp.max(p, axis=0, keepdims=True) for p in parts])
        ps = [jnp.exp2(p - m) for p in parts]
        l = functools.reduce(jnp.add, [jnp.sum(p, axis=0, keepdims=True) for p in ps])
        pb = jnp.concatenate([p.astype(BF16) for p in ps], axis=0)
        o_ref[qs, :] = _sigmoid(gb_ref[qs, :]) * (_dot(vt[:, :(j + 1) * blk], pb) / l).T


def _moba_prompt(proj, k_new, v_new, batch, seq, q_col_block, gb_col_block):
    n, d = k_new.shape
    heads = d // HEAD_DIM
    blk = MOBA_BLOCK
    nb = seq // blk
    seq_head = lambda cb: pl.BlockSpec((seq, HEAD_DIM), lambda b, h: (b, cb * heads + h))
    return pl.pallas_call(
        functools.partial(_moba_prompt_kernel, nb=nb, blk=blk),
        grid=(batch, heads),
        in_specs=[seq_head(q_col_block), seq_head(0), seq_head(0), seq_head(gb_col_block)],
        out_specs=seq_head(0),
        out_shape=jax.ShapeDtypeStruct((n, d), F32),
        compiler_params=_cparams(("parallel", "parallel")),
        name="moba_prompt",
    )(proj, k_new, v_new, proj)


def _block_diag_queries(q, heads, rows):
    d = q.shape[1]
    tiled = jnp.concatenate([q] * heads, axis=0)
    r = lax.broadcasted_iota(jnp.int32, (heads * rows, d), 0) // rows
    c = lax.broadcasted_iota(jnp.int32, (heads * rows, d), 1) // HEAD_DIM
    return jnp.where(r == c, tiled, 0.0)


def _moba_sample_kernel(pt_ref, q_ref, kn_ref, vn_ref, gb_ref, *refs, heads, rows, page, pages_per_step):
    del pt_ref
    k_refs, v_refs = refs[:pages_per_step], refs[pages_per_step:2 * pages_per_step]
    o_ref, lg_scr, ksum_scr, kmean_scr, bias_scr, hmask_scr, m_scr, l_scr, acc_scr = refs[2 * pages_per_step:]
    phase, g = pl.program_id(1), pl.program_id(2)
    blocks = range(pages_per_step // 2)
    hr = heads * rows
    d = heads * HEAD_DIM
    nb = bias_scr.shape[0]
    head = lambda h: slice(h * HEAD_DIM, (h + 1) * HEAD_DIM)
    pos_head = lambda ref: ref[...].reshape(page * heads, HEAD_DIM).astype(BF16)
    side_by_side = lambda refs, i: jnp.concatenate([pos_head(refs[2 * i]), pos_head(refs[2 * i + 1])], axis=1)

    q = q_ref[...]
    qs = q * SOFTMAX_LOG2_SCALE

    @pl.when(phase == 0)
    def _():
        for i, k_ref in enumerate(k_refs):
            ksum_scr[g * pages_per_step + i] = jnp.sum(k_ref[...], axis=0)
        q_hd = jnp.concatenate([qs[:, head(h)] for h in range(heads)], axis=0).astype(BF16)
        zero = jnp.zeros_like(q_hd)
        q_pair = jnp.concatenate([jnp.concatenate([q_hd, zero], axis=1),
                                  jnp.concatenate([zero, q_hd], axis=1)], axis=0)
        for i in blocks:
            lg_scr[g * len(blocks) + i] = _dot_nt(q_pair, side_by_side(k_refs, i))

    @pl.when((phase == 1) & (g == 0))
    def _():
        for n in range(nb):
            kmean_scr[n * heads:(n + 1) * heads, :] = (ksum_scr[2 * n] + ksum_scr[2 * n + 1]) * (0.5 / page)
        s = jnp.concatenate([_dot_nt_split(q[:, head(h)], kmean_scr[pl.ds(h, nb, stride=heads), :], F32)
                             for h in range(heads)], axis=0)
        cnt = jnp.zeros((hr, nb), jnp.int32)
        idx = lax.broadcasted_iota(jnp.int32, (hr, nb), 1)
        for m in range(nb):
            sm = s[:, m:m + 1]
            beats = (sm > s) | ((sm == s) & (idx > m))
            cnt = cnt + jnp.where(beats, 1, 0)
        bias = jnp.where(cnt < MOBA_TOPK, 0.0, -jnp.inf)
        for n in range(nb):
            col = jnp.broadcast_to(bias[:, n:n + 1], (hr, HEAD_DIM))
            bias_scr[n] = jnp.concatenate([col, col], axis=0)
        r = lax.broadcasted_iota(jnp.int32, hmask_scr.shape, 0)
        c = lax.broadcasted_iota(jnp.int32, hmask_scr.shape, 1)
        hmask_scr[...] = jnp.where(c % heads == (r % hr) // rows, 0.0, -jnp.inf)

        qbd = _block_diag_queries(qs, heads, rows).astype(BF16)
        kn = jnp.concatenate([kn_ref[...], jnp.zeros((page - rows, d), F32)], axis=0)
        vn = jnp.concatenate([vn_ref[...], jnp.zeros((page - rows, d), F32)], axis=0)
        r_own = lax.broadcasted_iota(jnp.int32, (hr, page), 0) % rows
        c_own = lax.broadcasted_iota(jnp.int32, (hr, page), 1)
        lg_own = jnp.where(c_own <= r_own, _dot_nt(qbd, kn.astype(BF16)), -jnp.inf)
        m0 = jnp.max(lg_own, axis=1, keepdims=True)
        p_own = jnp.exp2(lg_own - m0)
        pv = _dot(p_own.astype(BF16), vn.astype(BF16))
        m_scr[...] = m0
        l_scr[...] = jnp.sum(p_own, axis=1, keepdims=True)
        acc_scr[...] = jnp.concatenate([pv[h * rows:(h + 1) * rows, head(h)] for h in range(heads)], axis=0)

    @pl.when(phase == 1)
    def _():
        hmask = hmask_scr[...]
        lgs = [lg_scr[g * len(blocks) + i] + hmask
               + jnp.concatenate([bias_scr[g * len(blocks) + i]] * heads, axis=1) for i in blocks]
        m_row = functools.reduce(jnp.maximum, [jnp.max(lg, axis=1, keepdims=True) for lg in lgs])
        m_old = m_scr[...]
        m_new = jnp.maximum(m_old, jnp.maximum(m_row[:hr], m_row[hr:]))
        alpha = jnp.exp2(m_old - m_new)
        m_pair = jnp.concatenate([m_new, m_new], axis=0)
        ps = [jnp.exp2(lg - m_pair) for lg in lgs]
        l_row = functools.reduce(jnp.add, [jnp.sum(p, axis=1, keepdims=True) for p in ps])
        pv = _dot(jnp.concatenate([p.astype(BF16) for p in ps], axis=1),
                  jnp.concatenate([side_by_side(v_refs, i) for i in blocks], axis=0))
        m_scr[...] = m_new
        l_scr[...] = alpha * l_scr[...] + l_row[:hr] + l_row[hr:]
        acc_scr[...] = alpha * acc_scr[...] + pv[:hr, :HEAD_DIM] + pv[hr:, HEAD_DIM:]

    @pl.when((phase == 1) & (g == pl.num_programs(2) - 1))
    def _():
        out = acc_scr[...] / l_scr[...]
        for h in range(heads):
            o_ref[:, head(h)] = _sigmoid(gb_ref[:, head(h)]) * out[h * rows:(h + 1) * rows, :]


def _moba_sample(proj, k_new, v_new, cache_k, cache_v, page_table, layer, rows, q_col_block, gb_col_block):
    n, d = k_new.shape
    _, _, page, heads, dh = cache_k.shape
    batch, n_pages = page_table.shape
    assert dh == HEAD_DIM and 2 * page == MOBA_BLOCK and n_pages % 2 == 0
    nb = n_pages // 2
    hr = heads * rows
    pt = page_table.reshape(-1)

    pps = 2 * math.gcd(nb, 4)
    steps = n_pages // pps

    def page_spec(step_of, i):
        return pl.BlockSpec((None, None, page, heads, dh),
                            lambda b, ph, g, pt: (layer, pt[b * n_pages + step_of(ph, g) * pps + i], 0, 0, 0))

    k_step = lambda ph, g: jnp.where(ph == 0, g, steps - 1)
    v_step = lambda ph, g: jnp.where(ph == 0, 0, g)
    row_spec = lambda cb: pl.BlockSpec((rows, d), lambda b, ph, g, pt: (b, cb))
    return pl.pallas_call(
        functools.partial(_moba_sample_kernel, heads=heads, rows=rows, page=page, pages_per_step=pps),
        grid_spec=pltpu.PrefetchScalarGridSpec(
            num_scalar_prefetch=1,
            grid=(batch, 2, steps),
            in_specs=[row_spec(q_col_block), row_spec(0), row_spec(0), row_spec(gb_col_block)]
                     + [page_spec(k_step, i) for i in range(pps)] + [page_spec(v_step, i) for i in range(pps)],
            out_specs=row_spec(0),
            scratch_shapes=[pltpu.VMEM((nb, 2 * hr, page * heads), F32),
                            pltpu.VMEM((n_pages, heads, dh), F32),
                            pltpu.VMEM((nb * heads, dh), F32),
                            pltpu.VMEM((nb, 2 * hr, dh), F32),
                            pltpu.VMEM((2 * hr, page * heads), F32),
                            pltpu.VMEM((hr, 1), F32), pltpu.VMEM((hr, 1), F32), pltpu.VMEM((hr, dh), F32)],
        ),
        out_shape=jax.ShapeDtypeStruct((n, d), F32),
        compiler_params=_cparams(("parallel", "arbitrary", "arbitrary")),
        name="moba_sample",
    )(pt, proj, k_new, v_new, proj, *([cache_k] * pps), *([cache_v] * pps))


def _merge_rows(gated_a, gated_b, x, w_out, g_post, g_pre):
    m = _dot((gated_a + gated_b).astype(BF16), w_out)
    x1 = x + _rms(m, g_post)
    return x1, _rms(x1, g_pre).astype(BF16)


def _merge_kernel(a_ref, b_ref, x_ref, w_ref, gpost_ref, gpre_ref, x1_ref, h2_ref):
    x1_ref[...], h2_ref[...] = _merge_rows(a_ref[...], b_ref[...], x_ref[...], w_ref[...],
                                           gpost_ref[...], gpre_ref[...])


def _merge(gated_a, gated_b, x, w_out, g_post, g_pre, tm):
    n, d = x.shape
    row = pl.BlockSpec((tm, d), lambda i: (i, 0))
    vec = pl.BlockSpec((1, d), lambda i: (0, 0))
    return pl.pallas_call(
        _merge_kernel,
        grid=(n // tm,),
        in_specs=[row, row, row, pl.BlockSpec((d, d), lambda i: (0, 0)), vec, vec],
        out_specs=[row, row],
        out_shape=[jax.ShapeDtypeStruct((n, d), F32), jax.ShapeDtypeStruct((n, d), BF16)],
        compiler_params=_cparams(("parallel",)),
        name="merge",
    )(gated_a, gated_b, x, w_out, g_post, g_pre)


def _gelu_tanh(x):
    return 0.5 * x * (1.0 + jnp.tanh(math.sqrt(2.0 / math.pi) * (x + 0.044715 * (x * x * x))))


def _ffn_prompt_kernel(a_ref, b_ref, x_ref, wout_ref, gpost_ref, gpre_ref, wup_ref, cw_ref, cb_ref, wdn_ref,
                       g_ref, y_ref, st_ref, carry_scr, win_scr, act_scr, *, d_ff, cs):
    tm = x_ref.shape[0]
    x1, h = _merge_rows(a_ref[...], b_ref[...], x_ref[...], wout_ref[...], gpost_ref[...], gpre_ref[...])
    halo = carry_scr.shape[0]

    @pl.when(pl.program_id(1) == 0)
    def _():
        carry_scr[...] = jnp.zeros_like(carry_scr)

    def conv(col0, win):
        cols = slice(col0, col0 + cs)
        up = _dot(h, wup_ref[:, cols])
        win[0:halo, :] = carry_scr[:, cols]
        win[halo:, :] = up
        carry_scr[:, cols] = up[tm - halo:, :]
        st_ref[0, :, cols] = up[tm - (CONV_W - 1):, :]
        cw = cw_ref[:, cols]
        return (cb_ref[:, cols] + cw[0:1, :] * win[halo - 2:halo - 2 + tm, :]
                + cw[1:2, :] * win[halo - 1:halo - 1 + tm, :] + cw[2:3, :] * up)

    for c in range(d_ff // cs):
        gate = conv(c * cs, win_scr.at[(2 * c) % win_scr.shape[0]])
        lin = conv(d_ff + c * cs, win_scr.at[(2 * c + 1) % win_scr.shape[0]])
        act_scr[:, c * cs:(c + 1) * cs] = (_gelu_tanh(gate) * lin).astype(BF16)
    y_ref[...] = x1 + _rms(_dot(act_scr[...], wdn_ref[...]), g_ref[...])


def _ffn_sample_kernel(h_ref, x_ref, wg_ref, wl_ref, cwg_ref, cwl_ref, cbg_ref, cbl_ref, wdn_ref, g_ref,
                       sg_ref, sl_ref, y_ref, ng_ref, nl_ref, acc_scr):
    c = pl.program_id(0)
    h = h_ref[...]
    batch, _, cs = sg_ref.shape
    seq_rows = h.shape[0] // batch
    t = lax.broadcasted_iota(jnp.int32, (batch, seq_rows, cs), 1)

    @pl.when(c == 0)
    def _():
        acc_scr[...] = jnp.zeros_like(acc_scr)

    def conv(w_ref, s_ref, n_ref, cw_ref, cb_ref):
        up = _dot(h, w_ref[...])
        by_seq = lambda a: a.reshape(batch, seq_rows, cs)
        n_ref[...] = by_seq(up)[:, seq_rows - (CONV_W - 1):, :]
        prev2, prev1 = s_ref[:, 0:1, :], s_ref[:, 1:2, :]
        p1 = jnp.where(t == 0, prev1, by_seq(pltpu.roll(up, 1, 0)))
        p2 = jnp.where(t == 0, prev2, jnp.where(t == 1, prev1, by_seq(pltpu.roll(up, 2, 0))))
        cw = cw_ref[...]
        return (cb_ref[...] + cw[0:1, :] * p2 + cw[1:2, :] * p1 + cw[2:3, :] * by_seq(up)).reshape(up.shape)

    gate = conv(wg_ref, sg_ref, ng_ref, cwg_ref, cbg_ref)
    lin = conv(wl_ref, sl_ref, nl_ref, cwl_ref, cbl_ref)
    acc_scr[...] += _dot((_gelu_tanh(gate) * lin).astype(BF16), wdn_ref[...])

    @pl.when(c == pl.num_programs(0) - 1)
    def _():
        y_ref[...] = x_ref[...] + _rms(acc_scr[...], g_ref[...])


def _ffn_cs(d_ff):
    for cs in (512, 256, 128):
        if d_ff % cs == 0:
            return cs
    raise ValueError(f"d_ff={d_ff} is not a multiple of the 128-lane tile")


def _ffn_prompt(gated_a, gated_b, x, w_out, g_post, g_pre, w_up, conv_w, conv_b, w_down, gain, batch, seq, tm):
    n, d = x.shape
    d_ff = w_down.shape[0]
    n_t = seq // tm
    full = lambda a: pl.BlockSpec(a.shape, lambda b, t: (0, 0), pipeline_mode=pl.Buffered(1))
    row = pl.BlockSpec((tm, d), lambda b, t: (b * n_t + t, 0))
    cs = _ffn_cs(d_ff)
    halo = 8
    return pl.pallas_call(
        functools.partial(_ffn_prompt_kernel, d_ff=d_ff, cs=cs),
        grid=(batch, n_t),
        in_specs=[row, row, row, full(w_out), full(g_post), full(g_pre),
                  full(w_up), full(conv_w), full(conv_b), full(w_down), full(gain)],
        out_specs=[row, pl.BlockSpec((1, CONV_W - 1, 2 * d_ff), lambda b, t: (b, 0, 0))],
        out_shape=[jax.ShapeDtypeStruct((n, d), F32),
                   jax.ShapeDtypeStruct((batch, CONV_W - 1, 2 * d_ff), F32)],
        scratch_shapes=[pltpu.VMEM((halo, 2 * d_ff), F32), pltpu.VMEM((4, halo + tm, cs), F32),
                        pltpu.VMEM((tm, d_ff), BF16)],
        compiler_params=_cparams(("parallel", "arbitrary")),
        name="ffn_prompt",
    )(gated_a, gated_b, x, w_out, g_post, g_pre, w_up, conv_w, conv_b, w_down, gain)


def _ffn_sample(h2, x1, w_up, conv_w, conv_b, w_down, gain, state, seq):
    n, d = x1.shape
    d_ff = w_down.shape[0]
    batch = n // seq
    cs = _ffn_cs(d_ff)
    nc = d_ff // cs
    full = lambda a: pl.BlockSpec(a.shape, lambda c: (0, 0))
    gate = lambda rows: pl.BlockSpec((rows, cs), lambda c: (0, c))
    lin = lambda rows: pl.BlockSpec((rows, cs), lambda c: (0, nc + c))
    st_gate = pl.BlockSpec((batch, CONV_W - 1, cs), lambda c: (0, 0, c))
    st_lin = pl.BlockSpec((batch, CONV_W - 1, cs), lambda c: (0, 0, nc + c))
    half_state = jax.ShapeDtypeStruct((batch, CONV_W - 1, d_ff), F32)
    y, new_gate, new_lin = pl.pallas_call(
        _ffn_sample_kernel,
        grid=(nc,),
        in_specs=[full(h2), full(x1), gate(d), lin(d), gate(CONV_W), lin(CONV_W), gate(1), lin(1),
                  pl.BlockSpec((cs, d), lambda c: (c, 0)), full(gain), st_gate, st_lin],
        out_specs=[full(x1), st_gate, st_gate],
        out_shape=[jax.ShapeDtypeStruct((n, d), F32), half_state, half_state],
        scratch_shapes=[pltpu.VMEM((n, d), F32)],
        compiler_params=_cparams(("arbitrary",)),
        name="ffn_sample",
    )(h2, x1, w_up, w_up, conv_w, conv_w, conv_b, conv_b, w_down, gain, state, state)
    return y, jnp.concatenate([new_gate, new_lin], axis=-1)


def _row_tile(n, cap):
    t = min(n, cap)
    while n % t:
        t //= 2
    return t


def kernel(x_prompt, x_sample, cache_k, cache_v, page_table, state_hgrn, state_ffn_conv, norm_mix_pre, norm_mix_post, w_in, hgrn_lb_logits, hgrn_norm, w_out, norm_ffn_pre, norm_ffn_post, w_up, conv_w, conv_b, w_down):
    bp, tp, d = x_prompt.shape
    bs, ts, _ = x_sample.shape
    depth = w_in.shape[0]
    heads = d // HEAD_DIM
    n_pool, page = cache_k.shape[1], cache_k.shape[2]
    lb_all = jnp.cumsum(jax.nn.softmax(hgrn_lb_logits.astype(F32), axis=0), axis=0)

    xp = x_prompt.reshape(bp * tp, d)
    xs = x_sample.reshape(bs * ts, d)
    outs = [[] for _ in range(8)]
    for l in range(depth):
        w_in_b = w_in[l].astype(BF16)
        w_out_b = w_out[l].astype(BF16)
        w_up_b = w_up[l].astype(BF16)
        w_down_b = w_down[l].astype(BF16)
        vec = lambda a: a[l].reshape(1, -1)
        lb = lb_all[l].reshape(1, d)
        Q_BLK, GA_BLK, GB_BLK = 4, 5, 6

        def mixers(x, batch, seq, s0, moba):
            proj, k_new, v_new = _inproj(x, vec(norm_mix_pre), w_in_b, _row_tile(x.shape[0], 1024), d)
            gated_a, s_new = _hgrn(proj, lb, vec(hgrn_norm), s0, batch, seq, _row_tile(seq, 256), GA_BLK)
            return gated_a, moba(proj, k_new, v_new), k_new, v_new, s_new

        s0p = jnp.zeros((bp, heads, HEAD_DIM, HEAD_DIM), F32)
        ga, gb, kp, vp, sp = mixers(xp, bp, tp, s0p,
                                    lambda pr, k, v: _moba_prompt(pr, k, v, bp, tp, Q_BLK, GB_BLK))
        xp, cp = _ffn_prompt(ga, gb, xp, w_out_b, vec(norm_mix_post), vec(norm_ffn_pre), w_up_b, conv_w[l],
                             conv_b[l].reshape(1, -1), w_down_b, vec(norm_ffn_post), bp, tp, _row_tile(tp, 512))
        ga, gb, kn, vn, sn = mixers(xs, bs, ts, state_hgrn[l],
                                    lambda pr, k, v: _moba_sample(pr, k, v, cache_k, cache_v, page_table, l, ts,
                                                                  Q_BLK, GB_BLK))
        x1, h2 = _merge(ga, gb, xs, w_out_b, vec(norm_mix_post), vec(norm_ffn_pre), _row_tile(bs * ts, 512))
        xs, cn = _ffn_sample(h2, x1, w_up_b, conv_w[l], conv_b[l].reshape(1, -1), w_down_b,
                             vec(norm_ffn_post), state_ffn_conv[l], ts)
        kv = lambda a, b, t: a.reshape(b, t, heads, HEAD_DIM)
        for o, val in zip(outs, (kv(kp, bp, tp), kv(vp, bp, tp), sp, cp, kv(kn, bs, ts), kv(vn, bs, ts), sn, cn)):
            o.append(val)
    return (xp.reshape(bp, tp, d), xs.reshape(bs, ts, d), *(jnp.stack(o) for o in outs))
```

```python
import functools
import math

import jax
import jax.numpy as jnp
from jax import lax
from jax.experimental import pallas as pl
from jax.experimental.pallas import tpu as pltpu

F32 = jnp.float32
BF16 = jnp.bfloat16

NORM_EPS = 1e-6
HEAD_DIM = 128
MOBA_BLOCK = 256
MOBA_TOPK = 3
HGRN_CHUNK = 64
CONV_W = 3
BF16_ROWS = 16
SOFTMAX_LOG2_SCALE = math.log2(math.e) / math.sqrt(HEAD_DIM)
VMEM_LIMIT = 56 * 1024 * 1024


def _cparams(sem):
    return pltpu.CompilerParams(dimension_semantics=sem, vmem_limit_bytes=VMEM_LIMIT)


def _sigmoid(x):
    return 0.5 * jnp.tanh(0.5 * x) + 0.5


def _rms(x, gain):
    return x * lax.rsqrt(jnp.mean(x * x, axis=-1, keepdims=True) + NORM_EPS) * gain


def _dot(a, b):
    return jnp.dot(a, b, preferred_element_type=F32)


def _dot_nt(a, b):
    return lax.dot_general(a, b, (((1,), (1,)), ((), ())), preferred_element_type=F32)


def _dot_tn(a, b):
    return lax.dot_general(a, b, (((0,), (0,)), ((), ())), preferred_element_type=F32)


def _split_bf16(x, parts):
    out = []
    for _ in range(parts):
        p = x.astype(BF16)
        out.append(p)
        x = x - p.astype(F32)
    return out


def _dot_nt_split(a, b, operand_dtype=BF16):
    ah, al = (p.astype(operand_dtype) for p in _split_bf16(a, 2))
    bh, bl = (p.astype(operand_dtype) for p in _split_bf16(b, 2))
    return _dot_nt(ah, bh) + _dot_nt(ah, bl) + _dot_nt(al, bh)


def _inproj_kernel(x_ref, g_ref, w_ref, *refs, bounds):
    *out_refs, h_scr = refs
    j = pl.program_id(1)

    @pl.when(j == 0)
    def _():
        h_scr[...] = _rms(x_ref[...], g_ref[...]).astype(BF16)

    for o_ref, (lo, hi) in zip(out_refs, bounds):
        @pl.when((j >= lo) & (j < hi))
        def _():
            o_ref[...] = _dot(h_scr[...], w_ref[...]).astype(o_ref.dtype)


GATE_BLOCKS = (0, 2, 3, 7, 8)
WIDE_BLOCKS = (1, 4)
HQ, HI, HG, GA, GB = range(5)
HF, AQ = range(2)


def _inproj(x, gain, w_in, tm, tn, gate_dtype):
    n, d = x.shape
    per = d // tn
    groups = (GATE_BLOCKS, WIDE_BLOCKS, (5,), (6,))
    order = [b for g in groups for b in g]
    starts = [sum(len(g) for g in groups[:k]) * per for k in range(len(groups))]
    bounds = [(s, s + len(g) * per) for s, g in zip(starts, groups)]
    src_block = lambda jb: functools.reduce(jnp.add, [jnp.where(jb == k, b, 0) for k, b in enumerate(order)])
    out_spec = lambda lo, hi: pl.BlockSpec((tm, tn), lambda i, j: (i, jnp.clip(j - lo, 0, hi - lo - 1)))
    return pl.pallas_call(
        functools.partial(_inproj_kernel, bounds=bounds),
        grid=(n // tm, len(order) * per),
        in_specs=[
            pl.BlockSpec((tm, d), lambda i, j: (i, 0)),
            pl.BlockSpec((1, d), lambda i, j: (0, 0)),
            pl.BlockSpec((d, tn), lambda i, j: (0, src_block(j // per) * per + j % per)),
        ],
        out_specs=[out_spec(lo, hi) for lo, hi in bounds],
        out_shape=[jax.ShapeDtypeStruct((n, len(g) * d), dt)
                   for g, dt in zip(groups, (gate_dtype, F32, F32, F32))],
        scratch_shapes=[pltpu.VMEM((tm, d), BF16)],
        compiler_params=_cparams(("parallel", "arbitrary")),
        name="inproj",
    )(x, gain, w_in)


def _hgrn_kernel(hq_ref, hf_ref, hi_ref, hg_ref, ga_ref, lb_ref, gain_ref, s0_ref, a_ref, sout_ref, st_scr,
                 *, heads):
    tt = pl.program_id(1)
    C = HGRN_CHUNK
    assert 2 * C == HEAD_DIM
    rows = hq_ref.shape[0]
    d = heads * HEAD_DIM
    pad = (-rows) % C
    t = rows + pad

    @pl.when(tt == 0)
    def _():
        for h in range(heads):
            st_scr[h] = s0_ref[0, h].T

    lb = lb_ref[...]
    gain = gain_ref[...]
    tril = jnp.where(lax.broadcasted_iota(jnp.int32, (C, C), 0) >= lax.broadcasted_iota(jnp.int32, (C, C), 1),
                     1.0, 0.0).astype(BF16)
    causal = (lax.broadcasted_iota(jnp.int32, (C, HEAD_DIM), 1) <= lax.broadcasted_iota(jnp.int32, (C, HEAD_DIM), 0))
    zero_rows = jnp.zeros((C, HEAD_DIM), BF16)
    states = [st_scr[h] for h in range(heads)]

    def chunk_operands(c):
        live = min(rows, (c + 1) * C) - c * C
        rs = slice(c * C, c * C + live)

        def padded(x, fill):
            return jnp.concatenate([x, jnp.full((C - live, d), fill, F32)], axis=0) if live < C else x

        f = padded(lb + (1.0 - lb) * _sigmoid(hf_ref[rs, :]), 1.0)
        hq = hq_ref[rs, :].astype(F32)
        q = padded(hq * _sigmoid(hq), 0.0)
        vc = padded(hi_ref[rs, :].astype(F32), 0.0)
        b = functools.reduce(jnp.add, [_dot(tril, part) for part in _split_bf16(jnp.log(f), 3)])
        decay = jnp.exp(b[C - 1:C])
        k_scaled = (1.0 - f) * jnp.exp(-b)
        qt = (q * jnp.exp(b)).astype(BF16)
        kh = (k_scaled * decay).astype(BF16)
        return rs, live, decay, qt, k_scaled.astype(BF16), kh, vc, vc.astype(BF16)

    n_chunks = t // C
    ahead = chunk_operands(0)
    for c in range(n_chunks):
        rs, live, decay, qt, kt, kh, vc, vb = ahead
        if c + 1 < n_chunks:
            ahead = chunk_operands(c + 1)
        sls = [slice(h * HEAD_DIM, (h + 1) * HEAD_DIM) for h in range(heads)]
        out1 = [_dot_nt(qt[:, sl], jnp.concatenate([st.astype(BF16), kt[:, sl], zero_rows], axis=0))
                for sl, st in zip(sls, states)]
        v_t = [jnp.concatenate([jnp.zeros((C, HEAD_DIM), F32), vc[:, sl]], axis=0).T.astype(BF16) for sl in sls]
        scores = [jnp.where(causal, o1[:, HEAD_DIM:], 0.0).astype(BF16) for o1 in out1]
        out2 = [_dot(jnp.concatenate([sc, vt], axis=0), jnp.concatenate([vb[:, sl], kh[:, sl]], axis=0))
                for sc, vt, sl in zip(scores, v_t, sls)]
        states = [st * decay[:, sl] + o2[C:] for st, o2, sl in zip(states, out2, sls)]
        normed = jnp.concatenate([_rms(o1[:, :HEAD_DIM] + o2[:C], gain[:, sl])
                                  for o1, o2, sl in zip(out1, out2, sls)], axis=1)
        hg = hg_ref[rs, :].astype(F32)
        a_ref[rs, :] = _sigmoid(ga_ref[rs, :].astype(F32)) * (normed[:live] * (hg * _sigmoid(hg)))

    for h in range(heads):
        st_scr[h] = states[h]

    @pl.when(tt == pl.num_programs(1) - 1)
    def _():
        for h in range(heads):
            sout_ref[0, h] = st_scr[h].T


def _hgrn(gates, wide, lb, gain, s0, batch, seq, t_tile):
    n = gates.shape[0]
    d = lb.shape[1]
    heads = d // HEAD_DIM
    n_t = seq // t_tile
    col_spec = lambda cb: pl.BlockSpec((t_tile, d), lambda b, t: (b * n_t + t, cb))
    state_spec = pl.BlockSpec((1, heads, HEAD_DIM, HEAD_DIM), lambda b, t: (b, 0, 0, 0))
    return pl.pallas_call(
        functools.partial(_hgrn_kernel, heads=heads),
        grid=(batch, n_t),
        in_specs=[col_spec(HQ), col_spec(HF), col_spec(HI), col_spec(HG), col_spec(GA),
                  pl.BlockSpec((1, d), lambda b, t: (0, 0)),
                  pl.BlockSpec((1, d), lambda b, t: (0, 0)),
                  state_spec],
        out_specs=[pl.BlockSpec((t_tile, d), lambda b, t: (b * n_t + t, 0)), state_spec],
        out_shape=[jax.ShapeDtypeStruct((n, d), F32),
                   jax.ShapeDtypeStruct((batch, heads, HEAD_DIM, HEAD_DIM), F32)],
        scratch_shapes=[pltpu.VMEM((heads, HEAD_DIM, HEAD_DIM), F32)],
        compiler_params=_cparams(("parallel", "arbitrary")),
        name="hgrn",
    )(gates, wide, gates, gates, gates, lb, gain, s0)


def _topk_bias(s, valid, topk):
    n_blocks = len(valid)
    rows = []
    for n in range(n_blocks):
        sn = s[n:n + 1, :]
        cnt = jnp.zeros(sn.shape, jnp.int32)
        for m in range(n_blocks):
            if m == n:
                continue
            sm = s[m:m + 1, :]
            beats = (sm >= sn) if m < n else (sm > sn)
            cnt = cnt + jnp.where(beats & valid[m], 1, 0)
        rows.append(jnp.where(cnt < topk, 0.0, -jnp.inf))
    return rows


def _moba_prompt_kernel(q_ref, k_ref, v_ref, gb_ref, o_ref, *, nb, blk):
    t = nb * blk
    q = q_ref[...]
    k = k_ref[...]
    qb = (q * SOFTMAX_LOG2_SCALE).astype(BF16)
    kb = k.astype(BF16)
    vt = v_ref[...].T.astype(BF16)

    kmean = jnp.concatenate(
        [jnp.mean(k[n * blk:(n + 1) * blk, :], axis=0, keepdims=True) for n in range(nb)]
        + [jnp.zeros((BF16_ROWS - nb % BF16_ROWS, HEAD_DIM), F32)], axis=0)
    s = _dot_nt_split(kmean, q)
    q_blk = lax.broadcasted_iota(jnp.int32, (1, t), 1) // blk
    bias = _topk_bias(s, [q_blk > m for m in range(nb - 1)], MOBA_TOPK)

    key = lax.broadcasted_iota(jnp.int32, (blk, blk), 0)
    qry = lax.broadcasted_iota(jnp.int32, (blk, blk), 1)
    logits = lambda j: _dot_nt(kb[:(j + 1) * blk, :], qb[j * blk:(j + 1) * blk, :])
    lg_next = logits(nb - 1)
    for j in reversed(range(nb)):
        qs = slice(j * blk, (j + 1) * blk)
        lg = lg_next
        if j > 0:
            lg_next = logits(j - 1)
        parts = [lg[n * blk:(n + 1) * blk, :] + bias[n][:, qs] for n in range(j)]
        parts.append(jnp.where(key <= qry, lg[j * blk:, :], -jnp.inf))
        m = functools.reduce(jnp.maximum, [jnp.max(p, axis=0, keepdims=True) for p in parts])
        ps = [jnp.exp2(p - m) for p in parts]
        l = functools.reduce(jnp.add, [jnp.sum(p, axis=0, keepdims=True) for p in ps])
        pb = jnp.concatenate([p.astype(BF16) for p in ps], axis=0)
        o_ref[qs, :] = _sigmoid(gb_ref[qs, :].astype(F32)) * (_dot(vt[:, :(j + 1) * blk], pb) / l).T


def _moba_prompt(gates, wide, k_new, v_new, batch, seq):
    n, d = k_new.shape
    heads = d // HEAD_DIM
    blk = MOBA_BLOCK
    nb = seq // blk
    seq_head = lambda cb: pl.BlockSpec((seq, HEAD_DIM), lambda b, h: (b, cb * heads + h))
    return pl.pallas_call(
        functools.partial(_moba_prompt_kernel, nb=nb, blk=blk),
        grid=(batch, heads),
        in_specs=[seq_head(AQ), seq_head(0), seq_head(0), seq_head(GB)],
        out_specs=seq_head(0),
        out_shape=jax.ShapeDtypeStruct((n, d), F32),
        compiler_params=_cparams(("parallel", "parallel")),
        name="moba_prompt",
    )(wide, k_new, v_new, gates)


def _block_diag_queries(q, heads, rows):
    d = q.shape[1]
    tiled = jnp.concatenate([q] * heads, axis=0)
    r = lax.broadcasted_iota(jnp.int32, (heads * rows, d), 0) // rows
    c = lax.broadcasted_iota(jnp.int32, (heads * rows, d), 1) // HEAD_DIM
    return jnp.where(r == c, tiled, 0.0)


def _moba_sample_kernel(pt_ref, q_ref, kn_ref, vn_ref, gb_ref, *refs, heads, rows, page, pages_per_step):
    del pt_ref
    k_refs, v_refs = refs[:pages_per_step], refs[pages_per_step:2 * pages_per_step]
    o_ref, lg_scr, ksum_scr, kmean_scr, bias_scr, hmask_scr, m_scr, l_scr, acc_scr = refs[2 * pages_per_step:]
    phase, g = pl.program_id(1), pl.program_id(2)
    blocks = range(pages_per_step // 2)
    hr = heads * rows
    d = heads * HEAD_DIM
    nb = bias_scr.shape[0]
    head = lambda h: slice(h * HEAD_DIM, (h + 1) * HEAD_DIM)
    pos_head = lambda ref: ref[...].reshape(page * heads, HEAD_DIM).astype(BF16)
    side_by_side = lambda refs, i: jnp.concatenate([pos_head(refs[2 * i]), pos_head(refs[2 * i + 1])], axis=1)

    q = q_ref[...]
    qs = q * SOFTMAX_LOG2_SCALE

    @pl.when(phase == 0)
    def _():
        for i, k_ref in enumerate(k_refs):
            ksum_scr[g * pages_per_step + i] = jnp.sum(k_ref[...], axis=0)
        q_hd = jnp.concatenate([qs[:, head(h)] for h in range(heads)], axis=0).astype(BF16)
        zero = jnp.zeros_like(q_hd)
        q_pair = jnp.concatenate([jnp.concatenate([q_hd, zero], axis=1),
                                  jnp.concatenate([zero, q_hd], axis=1)], axis=0)
        for i in blocks:
            lg_scr[g * len(blocks) + i] = _dot_nt(q_pair, side_by_side(k_refs, i))

    @pl.when((phase == 1) & (g == 0))
    def _():
        for n in range(nb):
            kmean_scr[n * heads:(n + 1) * heads, :] = (ksum_scr[2 * n] + ksum_scr[2 * n + 1]) * (0.5 / page)
        s = jnp.concatenate([_dot_nt_split(q[:, head(h)], kmean_scr[pl.ds(h, nb, stride=heads), :], F32)
                             for h in range(heads)], axis=0)
        cnt = jnp.zeros((hr, nb), jnp.int32)
        idx = lax.broadcasted_iota(jnp.int32, (hr, nb), 1)
        for m in range(nb):
            sm = s[:, m:m + 1]
            beats = (sm > s) | ((sm == s) & (idx > m))
            cnt = cnt + jnp.where(beats, 1, 0)
        bias = jnp.where(cnt < MOBA_TOPK, 0.0, -jnp.inf)
        for n in range(nb):
            col = jnp.broadcast_to(bias[:, n:n + 1], (hr, HEAD_DIM))
            bias_scr[n] = jnp.concatenate([col, col], axis=0)
        r = lax.broadcasted_iota(jnp.int32, hmask_scr.shape, 0)
        c = lax.broadcasted_iota(jnp.int32, hmask_scr.shape, 1)
        hmask_scr[...] = jnp.where(c % heads == (r % hr) // rows, 0.0, -jnp.inf)

        qbd = _block_diag_queries(qs, heads, rows).astype(BF16)
        kn = jnp.concatenate([kn_ref[...], jnp.zeros((page - rows, d), F32)], axis=0)
        vn = jnp.concatenate([vn_ref[...], jnp.zeros((page - rows, d), F32)], axis=0)
        r_own = lax.broadcasted_iota(jnp.int32, (hr, page), 0) % rows
        c_own = lax.broadcasted_iota(jnp.int32, (hr, page), 1)
        lg_own = jnp.where(c_own <= r_own, _dot_nt(qbd, kn.astype(BF16)), -jnp.inf)
        m0 = jnp.max(lg_own, axis=1, keepdims=True)
        p_own = jnp.exp2(lg_own - m0)
        pv = _dot(p_own.astype(BF16), vn.astype(BF16))
        m_scr[...] = m0
        l_scr[...] = jnp.sum(p_own, axis=1, keepdims=True)
        acc_scr[...] = jnp.concatenate([pv[h * rows:(h + 1) * rows, head(h)] for h in range(heads)], axis=0)

    @pl.when(phase == 1)
    def _():
        hmask = hmask_scr[...]
        lgs = [lg_scr[g * len(blocks) + i] + hmask
               + jnp.concatenate([bias_scr[g * len(blocks) + i]] * heads, axis=1) for i in blocks]
        m_row = functools.reduce(jnp.maximum, [jnp.max(lg, axis=1, keepdims=True) for lg in lgs])
        m_old = m_scr[...]
        m_new = jnp.maximum(m_old, jnp.maximum(m_row[:hr], m_row[hr:]))
        alpha = jnp.exp2(m_old - m_new)
        m_pair = jnp.concatenate([m_new, m_new], axis=0)
        ps = [jnp.exp2(lg - m_pair) for lg in lgs]
        l_row = functools.reduce(jnp.add, [jnp.sum(p, axis=1, keepdims=True) for p in ps])
        pv = _dot(jnp.concatenate([p.astype(BF16) for p in ps], axis=1),
                  jnp.concatenate([side_by_side(v_refs, i) for i in blocks], axis=0))
        m_scr[...] = m_new
        l_scr[...] = alpha * l_scr[...] + l_row[:hr] + l_row[hr:]
        acc_scr[...] = alpha * acc_scr[...] + pv[:hr, :HEAD_DIM] + pv[hr:, HEAD_DIM:]

    @pl.when((phase == 1) & (g == pl.num_programs(2) - 1))
    def _():
        out = acc_scr[...] / l_scr[...]
        for h in range(heads):
            o_ref[:, head(h)] = _sigmoid(gb_ref[:, head(h)].astype(F32)) * out[h * rows:(h + 1) * rows, :]


def _moba_sample(gates, wide, k_new, v_new, cache_k, cache_v, page_table, layer, rows):
    n, d = k_new.shape
    _, _, page, heads, dh = cache_k.shape
    batch, n_pages = page_table.shape
    assert dh == HEAD_DIM and 2 * page == MOBA_BLOCK and n_pages % 2 == 0
    nb = n_pages // 2
    hr = heads * rows
    pt = page_table.reshape(-1)

    pps = 2 * math.gcd(nb, 4)
    steps = n_pages // pps

    def page_spec(step_of, i):
        return pl.BlockSpec((None, None, page, heads, dh),
                            lambda b, ph, g, pt: (layer, pt[b * n_pages + step_of(ph, g) * pps + i], 0, 0, 0))

    k_step = lambda ph, g: jnp.where(ph == 0, g, steps - 1)
    v_step = lambda ph, g: jnp.where(ph == 0, 0, g)
    row_spec = lambda cb: pl.BlockSpec((rows, d), lambda b, ph, g, pt: (b, cb))
    return pl.pallas_call(
        functools.partial(_moba_sample_kernel, heads=heads, rows=rows, page=page, pages_per_step=pps),
        grid_spec=pltpu.PrefetchScalarGridSpec(
            num_scalar_prefetch=1,
            grid=(batch, 2, steps),
            in_specs=[row_spec(AQ), row_spec(0), row_spec(0), row_spec(GB)]
                     + [page_spec(k_step, i) for i in range(pps)] + [page_spec(v_step, i) for i in range(pps)],
            out_specs=row_spec(0),
            scratch_shapes=[pltpu.VMEM((nb, 2 * hr, page * heads), F32),
                            pltpu.VMEM((n_pages, heads, dh), F32),
                            pltpu.VMEM((nb * heads, dh), F32),
                            pltpu.VMEM((nb, 2 * hr, dh), F32),
                            pltpu.VMEM((2 * hr, page * heads), F32),
                            pltpu.VMEM((hr, 1), F32), pltpu.VMEM((hr, 1), F32), pltpu.VMEM((hr, dh), F32)],
        ),
        out_shape=jax.ShapeDtypeStruct((n, d), F32),
        compiler_params=_cparams(("parallel", "arbitrary", "arbitrary")),
        name="moba_sample",
    )(pt, wide, k_new, v_new, gates, *([cache_k] * pps), *([cache_v] * pps))


def _merge_rows(gated_a, gated_b, x, w_out, g_post, g_pre):
    m = _dot((gated_a + gated_b).astype(BF16), w_out)
    x1 = x + _rms(m, g_post)
    return x1, _rms(x1, g_pre).astype(BF16)


def _merge_kernel(a_ref, b_ref, x_ref, w_ref, gpost_ref, gpre_ref, x1_ref, h2_ref):
    x1_ref[...], h2_ref[...] = _merge_rows(a_ref[...], b_ref[...], x_ref[...], w_ref[...],
                                           gpost_ref[...], gpre_ref[...])


def _merge(gated_a, gated_b, x, w_out, g_post, g_pre, tm):
    n, d = x.shape
    row = pl.BlockSpec((tm, d), lambda i: (i, 0))
    vec = pl.BlockSpec((1, d), lambda i: (0, 0))
    return pl.pallas_call(
        _merge_kernel,
        grid=(n // tm,),
        in_specs=[row, row, row, pl.BlockSpec((d, d), lambda i: (0, 0)), vec, vec],
        out_specs=[row, row],
        out_shape=[jax.ShapeDtypeStruct((n, d), F32), jax.ShapeDtypeStruct((n, d), BF16)],
        compiler_params=_cparams(("parallel",)),
        name="merge",
    )(gated_a, gated_b, x, w_out, g_post, g_pre)


def _gelu_tanh(x):
    return 0.5 * x * (1.0 + jnp.tanh(math.sqrt(2.0 / math.pi) * (x + 0.044715 * (x * x * x))))


def _ffn_prompt_kernel(a_ref, b_ref, x_ref, wout_ref, gpost_ref, gpre_ref, wup_ref, cw_ref, cb_ref, wdn_ref,
                       g_ref, y_ref, st_ref, carry_scr, win_scr, act_scr, *, d_ff, cs):
    tm = x_ref.shape[0]
    x1, h = _merge_rows(a_ref[...], b_ref[...], x_ref[...], wout_ref[...], gpost_ref[...], gpre_ref[...])
    halo = carry_scr.shape[0]

    @pl.when(pl.program_id(1) == 0)
    def _():
        carry_scr[...] = jnp.zeros_like(carry_scr)

    def conv(col0, win):
        cols = slice(col0, col0 + cs)
        up = _dot(h, wup_ref[:, cols])
        win[0:halo, :] = carry_scr[:, cols]
        win[halo:, :] = up
        carry_scr[:, cols] = up[tm - halo:, :]
        st_ref[0, :, cols] = up[tm - (CONV_W - 1):, :]
        cw = cw_ref[:, cols]
        return (cb_ref[:, cols] + cw[0:1, :] * win[halo - 2:halo - 2 + tm, :]
                + cw[1:2, :] * win[halo - 1:halo - 1 + tm, :] + cw[2:3, :] * up)

    for c in range(d_ff // cs):
        gate = conv(c * cs, win_scr.at[(2 * c) % win_scr.shape[0]])
        lin = conv(d_ff + c * cs, win_scr.at[(2 * c + 1) % win_scr.shape[0]])
        act_scr[:, c * cs:(c + 1) * cs] = (_gelu_tanh(gate) * lin).astype(BF16)
    y_ref[...] = x1 + _rms(_dot(act_scr[...], wdn_ref[...]), g_ref[...])


def _ffn_sample_kernel(h_ref, x_ref, wg_ref, wl_ref, cwg_ref, cwl_ref, cbg_ref, cbl_ref, wdn_ref, g_ref,
                       sg_ref, sl_ref, y_ref, ng_ref, nl_ref, acc_scr):
    c = pl.program_id(0)
    h = h_ref[...]
    batch, _, cs = sg_ref.shape
    seq_rows = h.shape[0] // batch
    t = lax.broadcasted_iota(jnp.int32, (batch, seq_rows, cs), 1)

    @pl.when(c == 0)
    def _():
        acc_scr[...] = jnp.zeros_like(acc_scr)

    def conv(w_ref, s_ref, n_ref, cw_ref, cb_ref):
        up = _dot(h, w_ref[...])
        by_seq = lambda a: a.reshape(batch, seq_rows, cs)
        n_ref[...] = by_seq(up)[:, seq_rows - (CONV_W - 1):, :]
        prev2, prev1 = s_ref[:, 0:1, :], s_ref[:, 1:2, :]
        p1 = jnp.where(t == 0, prev1, by_seq(pltpu.roll(up, 1, 0)))
        p2 = jnp.where(t == 0, prev2, jnp.where(t == 1, prev1, by_seq(pltpu.roll(up, 2, 0))))
        cw = cw_ref[...]
        return (cb_ref[...] + cw[0:1, :] * p2 + cw[1:2, :] * p1 + cw[2:3, :] * by_seq(up)).reshape(up.shape)

    gate = conv(wg_ref, sg_ref, ng_ref, cwg_ref, cbg_ref)
    lin = conv(wl_ref, sl_ref, nl_ref, cwl_ref, cbl_ref)
    acc_scr[...] += _dot((_gelu_tanh(gate) * lin).astype(BF16), wdn_ref[...])

    @pl.when(c == pl.num_programs(0) - 1)
    def _():
        y_ref[...] = x_ref[...] + _rms(acc_scr[...], g_ref[...])


def _ffn_cs(d_ff):
    for cs in (512, 256, 128):
        if d_ff % cs == 0:
            return cs
    raise ValueError(f"d_ff={d_ff} is not a multiple of the 128-lane tile")


def _ffn_prompt(gated_a, gated_b, x, w_out, g_post, g_pre, w_up, conv_w, conv_b, w_down, gain, batch, seq, tm):
    n, d = x.shape
    d_ff = w_down.shape[0]
    n_t = seq // tm
    full = lambda a: pl.BlockSpec(a.shape, lambda b, t: (0, 0), pipeline_mode=pl.Buffered(1))
    row = pl.BlockSpec((tm, d), lambda b, t: (b * n_t + t, 0))
    cs = _ffn_cs(d_ff)
    halo = 8
    return pl.pallas_call(
        functools.partial(_ffn_prompt_kernel, d_ff=d_ff, cs=cs),
        grid=(batch, n_t),
        in_specs=[row, row, row, full(w_out), full(g_post), full(g_pre),
                  full(w_up), full(conv_w), full(conv_b), full(w_down), full(gain)],
        out_specs=[row, pl.BlockSpec((1, CONV_W - 1, 2 * d_ff), lambda b, t: (b, 0, 0))],
        out_shape=[jax.ShapeDtypeStruct((n, d), F32),
                   jax.ShapeDtypeStruct((batch, CONV_W - 1, 2 * d_ff), F32)],
        scratch_shapes=[pltpu.VMEM((halo, 2 * d_ff), F32), pltpu.VMEM((4, halo + tm, cs), F32),
                        pltpu.VMEM((tm, d_ff), BF16)],
        compiler_params=_cparams(("parallel", "arbitrary")),
        name="ffn_prompt",
    )(gated_a, gated_b, x, w_out, g_post, g_pre, w_up, conv_w, conv_b, w_down, gain)


def _ffn_sample(h2, x1, w_up, conv_w, conv_b, w_down, gain, state, seq):
    n, d = x1.shape
    d_ff = w_down.shape[0]
    batch = n // seq
    cs = _ffn_cs(d_ff)
    nc = d_ff // cs
    full = lambda a: pl.BlockSpec(a.shape, lambda c: (0, 0))
    gate = lambda rows: pl.BlockSpec((rows, cs), lambda c: (0, c))
    lin = lambda rows: pl.BlockSpec((rows, cs), lambda c: (0, nc + c))
    st_gate = pl.BlockSpec((batch, CONV_W - 1, cs), lambda c: (0, 0, c))
    st_lin = pl.BlockSpec((batch, CONV_W - 1, cs), lambda c: (0, 0, nc + c))
    half_state = jax.ShapeDtypeStruct((batch, CONV_W - 1, d_ff), F32)
    y, new_gate, new_lin = pl.pallas_call(
        _ffn_sample_kernel,
        grid=(nc,),
        in_specs=[full(h2), full(x1), gate(d), lin(d), gate(CONV_W), lin(CONV_W), gate(1), lin(1),
                  pl.BlockSpec((cs, d), lambda c: (c, 0)), full(gain), st_gate, st_lin],
        out_specs=[full(x1), st_gate, st_gate],
        out_shape=[jax.ShapeDtypeStruct((n, d), F32), half_state, half_state],
        scratch_shapes=[pltpu.VMEM((n, d), F32)],
        compiler_params=_cparams(("arbitrary",)),
        name="ffn_sample",
    )(h2, x1, w_up, w_up, conv_w, conv_w, conv_b, conv_b, w_down, gain, state, state)
    return y, jnp.concatenate([new_gate, new_lin], axis=-1)


def _row_tile(n, cap):
    t = min(n, cap)
    while n % t:
        t //= 2
    return t


def kernel(x_prompt, x_sample, cache_k, cache_v, page_table, state_hgrn, state_ffn_conv, norm_mix_pre, norm_mix_post, w_in, hgrn_lb_logits, hgrn_norm, w_out, norm_ffn_pre, norm_ffn_post, w_up, conv_w, conv_b, w_down):
    bp, tp, d = x_prompt.shape
    bs, ts, _ = x_sample.shape
    depth = w_in.shape[0]
    heads = d // HEAD_DIM
    n_pool, page = cache_k.shape[1], cache_k.shape[2]
    lb_all = jnp.cumsum(jax.nn.softmax(hgrn_lb_logits.astype(F32), axis=0), axis=0)

    xp = x_prompt.reshape(bp * tp, d)
    xs = x_sample.reshape(bs * ts, d)
    outs = [[] for _ in range(8)]
    for l in range(depth):
        w_in_b = w_in[l].astype(BF16)
        w_out_b = w_out[l].astype(BF16)
        w_up_b = w_up[l].astype(BF16)
        w_down_b = w_down[l].astype(BF16)
        vec = lambda a: a[l].reshape(1, -1)
        lb = lb_all[l].reshape(1, d)

        def mixers(x, batch, seq, s0, moba, gate_dtype):
            gates, wide, k_new, v_new = _inproj(x, vec(norm_mix_pre), w_in_b, _row_tile(x.shape[0], 1024), d,
                                                gate_dtype)
            gated_a, s_new = _hgrn(gates, wide, lb, vec(hgrn_norm), s0, batch, seq, _row_tile(seq, 256))
            return gated_a, moba(gates, wide, k_new, v_new), k_new, v_new, s_new

        s0p = jnp.zeros((bp, heads, HEAD_DIM, HEAD_DIM), F32)
        ga, gb, kp, vp, sp = mixers(xp, bp, tp, s0p, lambda g, w, k, v: _moba_prompt(g, w, k, v, bp, tp), BF16)
        xp, cp = _ffn_prompt(ga, gb, xp, w_out_b, vec(norm_mix_post), vec(norm_ffn_pre), w_up_b, conv_w[l],
                             conv_b[l].reshape(1, -1), w_down_b, vec(norm_ffn_post), bp, tp, _row_tile(tp, 512))
        ga, gb, kn, vn, sn = mixers(xs, bs, ts, state_hgrn[l],
                                    lambda g, w, k, v: _moba_sample(g, w, k, v, cache_k, cache_v, page_table, l, ts),
                                    F32)
        x1, h2 = _merge(ga, gb, xs, w_out_b, vec(norm_mix_post), vec(norm_ffn_pre), _row_tile(bs * ts, 512))
        xs, cn = _ffn_sample(h2, x1, w_up_b, conv_w[l], conv_b[l].reshape(1, -1), w_down_b,
                             vec(norm_ffn_post), state_ffn_conv[l], ts)
        kv = lambda a, b, t: a.reshape(b, t, heads, HEAD_DIM)
        for o, val in zip(outs, (kv(kp, bp, tp), kv(vp, bp, tp), sp, cp, kv(kn, bs, ts), kv(vn, bs, ts), sn, cn)):
            o.append(val)
    return (xp.reshape(bp, tp, d), xs.reshape(bs, ts, d), *(jnp.stack(o) for o in outs))
```

```python
import functools
import math

import jax
import jax.numpy as jnp
from jax import lax
from jax.experimental import pallas as pl
from jax.experimental.pallas import tpu as pltpu

F32 = jnp.float32
BF16 = jnp.bfloat16

NORM_EPS = 1e-6
HEAD_DIM = 128
MOBA_BLOCK = 256
MOBA_TOPK = 3
HGRN_CHUNK = 64
CONV_W = 3
BF16_ROWS = 16
MOBA_HEADS_PER_STEP = 4
SOFTMAX_LOG2_SCALE = math.log2(math.e) / math.sqrt(HEAD_DIM)
VMEM_LIMIT = 56 * 1024 * 1024


def _cparams(sem):
    return pltpu.CompilerParams(dimension_semantics=sem, vmem_limit_bytes=VMEM_LIMIT)


def _sigmoid(x):
    return 0.5 * jnp.tanh(0.5 * x) + 0.5


def _rms(x, gain):
    return x * lax.rsqrt(jnp.mean(x * x, axis=-1, keepdims=True) + NORM_EPS) * gain


def _dot(a, b):
    return jnp.dot(a, b, preferred_element_type=F32)


def _dot_nt(a, b):
    return lax.dot_general(a, b, (((1,), (1,)), ((), ())), preferred_element_type=F32)


def _split_bf16(x, parts):
    out = []
    for _ in range(parts):
        p = x.astype(BF16)
        out.append(p)
        x = x - p.astype(F32)
    return out


def _dot_nt_split(a, b, operand_dtype=BF16):
    ah, al = (p.astype(operand_dtype) for p in _split_bf16(a, 2))
    bh, bl = (p.astype(operand_dtype) for p in _split_bf16(b, 2))
    return _dot_nt(ah, bh) + _dot_nt(ah, bl) + _dot_nt(al, bh)


def _inproj_kernel(x_ref, g_ref, w_ref, p_ref, k_ref, v_ref, h_scr, *, main_steps, kv_steps):
    j = pl.program_id(1)

    @pl.when(j == 0)
    def _():
        h_scr[...] = _rms(x_ref[...], g_ref[...]).astype(BF16)

    @pl.when(j < main_steps)
    def _():
        p_ref[...] = _dot(h_scr[...], w_ref[...])

    @pl.when((j >= main_steps) & (j < main_steps + kv_steps))
    def _():
        k_ref[...] = _dot(h_scr[...], w_ref[...])

    @pl.when(j >= main_steps + kv_steps)
    def _():
        v_ref[...] = _dot(h_scr[...], w_ref[...])


def _inproj(x, gain, w_in, tm, tn):
    n, d = x.shape
    per = d // tn
    main_steps, kv_steps = 7 * per, per
    src_block = lambda jb: jnp.where(jb < 5, jb, jnp.where(jb < 7, jb + 2, jb - 2))
    return pl.pallas_call(
        functools.partial(_inproj_kernel, main_steps=main_steps, kv_steps=kv_steps),
        grid=(n // tm, 9 * per),
        in_specs=[
            pl.BlockSpec((tm, d), lambda i, j: (i, 0)),
            pl.BlockSpec((1, d), lambda i, j: (0, 0)),
            pl.BlockSpec((d, tn), lambda i, j: (0, src_block(j // per) * per + j % per)),
        ],
        out_specs=[
            pl.BlockSpec((tm, tn), lambda i, j: (i, jnp.minimum(j, main_steps - 1))),
            pl.BlockSpec((tm, tn), lambda i, j: (i, jnp.clip(j - main_steps, 0, kv_steps - 1))),
            pl.BlockSpec((tm, tn), lambda i, j: (i, jnp.clip(j - main_steps - kv_steps, 0, kv_steps - 1))),
        ],
        out_shape=[
            jax.ShapeDtypeStruct((n, 7 * d), F32),
            jax.ShapeDtypeStruct((n, d), F32),
            jax.ShapeDtypeStruct((n, d), F32),
        ],
        scratch_shapes=[pltpu.VMEM((tm, d), BF16)],
        compiler_params=_cparams(("parallel", "arbitrary")),
        name="inproj",
    )(x, gain, w_in)


def _hgrn_kernel(hq_ref, hf_ref, hi_ref, hg_ref, ga_ref, lb_ref, gain_ref, s0_ref, a_ref, sout_ref, st_scr,
                 *, heads):
    tt = pl.program_id(1)
    C = HGRN_CHUNK
    assert 2 * C == HEAD_DIM
    rows = hq_ref.shape[0]
    d = heads * HEAD_DIM
    n_chunks = -(-rows // C)

    @pl.when(tt == 0)
    def _():
        for h in range(heads):
            st_scr[h] = s0_ref[0, h].T

    lb = lb_ref[...]
    gain = gain_ref[...]
    tril = jnp.where(lax.broadcasted_iota(jnp.int32, (C, C), 0) >= lax.broadcasted_iota(jnp.int32, (C, C), 1),
                     1.0, 0.0).astype(BF16)
    causal = (lax.broadcasted_iota(jnp.int32, (C, HEAD_DIM), 1) <= lax.broadcasted_iota(jnp.int32, (C, HEAD_DIM), 0))
    zero_rows = jnp.zeros((C, HEAD_DIM), BF16)
    states = [st_scr[h] for h in range(heads)]

    def chunk_operands(c):
        live = min(rows, (c + 1) * C) - c * C
        rs = slice(c * C, c * C + live)

        def padded(x, fill):
            return jnp.concatenate([x, jnp.full((C - live, d), fill, F32)], axis=0) if live < C else x

        f = padded(lb + (1.0 - lb) * _sigmoid(hf_ref[rs, :]), 1.0)
        hq = hq_ref[rs, :]
        q = padded(hq * _sigmoid(hq), 0.0)
        vc = padded(hi_ref[rs, :], 0.0)
        b = functools.reduce(jnp.add, [_dot(tril, part) for part in _split_bf16(jnp.log(f), 3)])
        decay = jnp.exp(b[C - 1:C])
        k_scaled = (1.0 - f) * jnp.exp(-b)
        qt = (q * jnp.exp(b)).astype(BF16)
        kh = (k_scaled * decay).astype(BF16)
        return rs, live, decay, qt, k_scaled.astype(BF16), kh, vc, vc.astype(BF16)

    ahead = chunk_operands(0)
    for c in range(n_chunks):
        rs, live, decay, qt, kt, kh, vc, vb = ahead
        if c + 1 < n_chunks:
            ahead = chunk_operands(c + 1)
        sls = [slice(h * HEAD_DIM, (h + 1) * HEAD_DIM) for h in range(heads)]
        out1 = [_dot_nt(qt[:, sl], jnp.concatenate([st.astype(BF16), kt[:, sl], zero_rows], axis=0))
                for sl, st in zip(sls, states)]
        v_t = [jnp.concatenate([jnp.zeros((C, HEAD_DIM), F32), vc[:, sl]], axis=0).T.astype(BF16) for sl in sls]
        scores = [jnp.where(causal, o1[:, HEAD_DIM:], 0.0).astype(BF16) for o1 in out1]
        out2 = [_dot(jnp.concatenate([sc, vt], axis=0), jnp.concatenate([vb[:, sl], kh[:, sl]], axis=0))
                for sc, vt, sl in zip(scores, v_t, sls)]
        states = [st * decay[:, sl] + o2[C:] for st, o2, sl in zip(states, out2, sls)]
        normed = jnp.concatenate([_rms(o1[:, :HEAD_DIM] + o2[:C], gain[:, sl])
                                  for o1, o2, sl in zip(out1, out2, sls)], axis=1)
        hg = hg_ref[rs, :]
        a_ref[rs, :] = _sigmoid(ga_ref[rs, :]) * (normed[:live] * (hg * _sigmoid(hg)))

    for h in range(heads):
        st_scr[h] = states[h]

    @pl.when(tt == pl.num_programs(1) - 1)
    def _():
        for h in range(heads):
            sout_ref[0, h] = st_scr[h].T


def _hgrn(proj, lb, gain, s0, batch, seq, t_tile, ga_block):
    n = proj.shape[0]
    d = lb.shape[1]
    heads = d // HEAD_DIM
    n_t = seq // t_tile
    col_spec = lambda cb: pl.BlockSpec((t_tile, d), lambda b, t: (b * n_t + t, cb))
    state_spec = pl.BlockSpec((1, heads, HEAD_DIM, HEAD_DIM), lambda b, t: (b, 0, 0, 0))
    return pl.pallas_call(
        functools.partial(_hgrn_kernel, heads=heads),
        grid=(batch, n_t),
        in_specs=[col_spec(0), col_spec(1), col_spec(2), col_spec(3), col_spec(ga_block),
                  pl.BlockSpec((1, d), lambda b, t: (0, 0)),
                  pl.BlockSpec((1, d), lambda b, t: (0, 0)),
                  state_spec],
        out_specs=[pl.BlockSpec((t_tile, d), lambda b, t: (b * n_t + t, 0)), state_spec],
        out_shape=[jax.ShapeDtypeStruct((n, d), F32),
                   jax.ShapeDtypeStruct((batch, heads, HEAD_DIM, HEAD_DIM), F32)],
        scratch_shapes=[pltpu.VMEM((heads, HEAD_DIM, HEAD_DIM), F32)],
        compiler_params=_cparams(("parallel", "arbitrary")),
        name="hgrn",
    )(proj, proj, proj, proj, proj, lb, gain, s0)


def _topk_bias(s, valid, topk):
    n_blocks = len(valid)
    rows = []
    for n in range(n_blocks):
        sn = s[n:n + 1, :]
        cnt = jnp.zeros(sn.shape, jnp.int32)
        for m in range(n_blocks):
            if m == n:
                continue
            sm = s[m:m + 1, :]
            beats = (sm >= sn) if m < n else (sm > sn)
            cnt = cnt + jnp.where(beats & valid[m], 1, 0)
        rows.append(jnp.where(cnt < topk, 0.0, -jnp.inf))
    return rows


def _moba_prompt_kernel(q_ref, k_ref, v_ref, gb_ref, o_ref, *, nb, blk):
    for i in range(q_ref.shape[1] // HEAD_DIM):
        _moba_prompt_head(q_ref, k_ref, v_ref, gb_ref, o_ref, slice(i * HEAD_DIM, (i + 1) * HEAD_DIM), nb, blk)


def _moba_prompt_head(q_ref, k_ref, v_ref, gb_ref, o_ref, hs, nb, blk):
    t = nb * blk
    q = q_ref[:, hs]
    k = k_ref[:, hs]
    qb = (q * SOFTMAX_LOG2_SCALE).astype(BF16)
    kb = k.astype(BF16)
    vt = v_ref[:, hs].T.astype(BF16)

    kmean = jnp.concatenate(
        [jnp.mean(k[n * blk:(n + 1) * blk, :], axis=0, keepdims=True) for n in range(nb)]
        + [jnp.zeros((BF16_ROWS - nb % BF16_ROWS, HEAD_DIM), F32)], axis=0)
    s = _dot_nt_split(kmean, q)
    q_blk = lax.broadcasted_iota(jnp.int32, (1, t), 1) // blk
    bias = _topk_bias(s, [q_blk > m for m in range(nb - 1)], MOBA_TOPK)

    key = lax.broadcasted_iota(jnp.int32, (blk, blk), 0)
    qry = lax.broadcasted_iota(jnp.int32, (blk, blk), 1)
    logits = lambda j: _dot_nt(kb[:(j + 1) * blk, :], qb[j * blk:(j + 1) * blk, :])
    lg_next = logits(nb - 1)
    for j in reversed(range(nb)):
        qs = slice(j * blk, (j + 1) * blk)
        lg = lg_next
        if j > 0:
            lg_next = logits(j - 1)
        parts = [lg[n * blk:(n + 1) * blk, :] + bias[n][:, qs] for n in range(j)]
        parts.append(jnp.where(key <= qry, lg[j * blk:, :], -jnp.inf))
        m = functools.reduce(jnp.maximum, [jnp.max(p, axis=0, keepdims=True) for p in parts])
        ps = [jnp.exp2(p - m) for p in parts]
        l = functools.reduce(jnp.add, [jnp.sum(p, axis=0, keepdims=True) for p in ps])
        pb = jnp.concatenate([p.astype(BF16) for p in ps], axis=0)
        o_ref[qs, hs] = _sigmoid(gb_ref[qs, hs]) * (_dot(vt[:, :(j + 1) * blk], pb) / l).T


def _moba_prompt(proj, k_new, v_new, batch, seq, q_col_block, gb_col_block):
    n, d = k_new.shape
    heads = d // HEAD_DIM
    blk = MOBA_BLOCK
    nb = seq // blk
    hps = math.gcd(heads, MOBA_HEADS_PER_STEP)
    seq_heads = lambda cb: pl.BlockSpec((seq, hps * HEAD_DIM), lambda b, h: (b, cb * (heads // hps) + h))
    return pl.pallas_call(
        functools.partial(_moba_prompt_kernel, nb=nb, blk=blk),
        grid=(batch, heads // hps),
        in_specs=[seq_heads(q_col_block), seq_heads(0), seq_heads(0), seq_heads(gb_col_block)],
        out_specs=seq_heads(0),
        out_shape=jax.ShapeDtypeStruct((n, d), F32),
        compiler_params=_cparams(("parallel", "parallel")),
        name="moba_prompt",
    )(proj, k_new, v_new, proj)


def _block_diag_queries(q, heads, rows):
    d = q.shape[1]
    tiled = jnp.concatenate([q] * heads, axis=0)
    r = lax.broadcasted_iota(jnp.int32, (heads * rows, d), 0) // rows
    c = lax.broadcasted_iota(jnp.int32, (heads * rows, d), 1) // HEAD_DIM
    return jnp.where(r == c, tiled, 0.0)


def _moba_sample_kernel(pt_ref, q_ref, kn_ref, vn_ref, gb_ref, *refs, heads, rows, page, pages_per_step):
    del pt_ref
    k_refs, v_refs = refs[:pages_per_step], refs[pages_per_step:2 * pages_per_step]
    o_ref, lg_scr, ksum_scr, kmean_scr, bias_scr, hmask_scr, m_scr, l_scr, acc_scr = refs[2 * pages_per_step:]
    phase, g = pl.program_id(1), pl.program_id(2)
    blocks = range(pages_per_step // 2)
    hr = heads * rows
    d = heads * HEAD_DIM
    nb = bias_scr.shape[0]
    head = lambda h: slice(h * HEAD_DIM, (h + 1) * HEAD_DIM)
    pos_head = lambda ref: ref[...].reshape(page * heads, HEAD_DIM).astype(BF16)
    side_by_side = lambda refs, i: jnp.concatenate([pos_head(refs[2 * i]), pos_head(refs[2 * i + 1])], axis=1)

    q = q_ref[...]
    qs = q * SOFTMAX_LOG2_SCALE

    @pl.when(phase == 0)
    def _():
        for i, k_ref in enumerate(k_refs):
            ksum_scr[g * pages_per_step + i] = jnp.sum(k_ref[...], axis=0)
        q_hd = jnp.concatenate([qs[:, head(h)] for h in range(heads)], axis=0).astype(BF16)
        zero = jnp.zeros_like(q_hd)
        q_pair = jnp.concatenate([jnp.concatenate([q_hd, zero], axis=1),
                                  jnp.concatenate([zero, q_hd], axis=1)], axis=0)
        for i in blocks:
            lg_scr[g * len(blocks) + i] = _dot_nt(q_pair, side_by_side(k_refs, i))

    @pl.when((phase == 1) & (g == 0))
    def _():
        for n in range(nb):
            kmean_scr[n * heads:(n + 1) * heads, :] = (ksum_scr[2 * n] + ksum_scr[2 * n + 1]) * (0.5 / page)
        s = jnp.concatenate([_dot_nt_split(q[:, head(h)], kmean_scr[pl.ds(h, nb, stride=heads), :], F32)
                             for h in range(heads)], axis=0)
        cnt = jnp.zeros((hr, nb), jnp.int32)
        idx = lax.broadcasted_iota(jnp.int32, (hr, nb), 1)
        for m in range(nb):
            sm = s[:, m:m + 1]
            beats = (sm > s) | ((sm == s) & (idx > m))
            cnt = cnt + jnp.where(beats, 1, 0)
        bias = jnp.where(cnt < MOBA_TOPK, 0.0, -jnp.inf)
        for n in range(nb):
            col = jnp.broadcast_to(bias[:, n:n + 1], (hr, HEAD_DIM))
            bias_scr[n] = jnp.concatenate([col, col], axis=0)
        r = lax.broadcasted_iota(jnp.int32, hmask_scr.shape, 0)
        c = lax.broadcasted_iota(jnp.int32, hmask_scr.shape, 1)
        hmask_scr[...] = jnp.where(c % heads == (r % hr) // rows, 0.0, -jnp.inf)

        qbd = _block_diag_queries(qs, heads, rows).astype(BF16)
        kn = jnp.concatenate([kn_ref[...], jnp.zeros((page - rows, d), F32)], axis=0)
        vn = jnp.concatenate([vn_ref[...], jnp.zeros((page - rows, d), F32)], axis=0)
        r_own = lax.broadcasted_iota(jnp.int32, (hr, page), 0) % rows
        c_own = lax.broadcasted_iota(jnp.int32, (hr, page), 1)
        lg_own = jnp.where(c_own <= r_own, _dot_nt(qbd, kn.astype(BF16)), -jnp.inf)
        m0 = jnp.max(lg_own, axis=1, keepdims=True)
        p_own = jnp.exp2(lg_own - m0)
        pv = _dot(p_own.astype(BF16), vn.astype(BF16))
        m_scr[...] = m0
        l_scr[...] = jnp.sum(p_own, axis=1, keepdims=True)
        acc_scr[...] = jnp.concatenate([pv[h * rows:(h + 1) * rows, head(h)] for h in range(heads)], axis=0)

    @pl.when(phase == 1)
    def _():
        hmask = hmask_scr[...]
        lgs = [lg_scr[g * len(blocks) + i] + hmask
               + jnp.concatenate([bias_scr[g * len(blocks) + i]] * heads, axis=1) for i in blocks]
        m_row = functools.reduce(jnp.maximum, [jnp.max(lg, axis=1, keepdims=True) for lg in lgs])
        m_old = m_scr[...]
        m_new = jnp.maximum(m_old, jnp.maximum(m_row[:hr], m_row[hr:]))
        alpha = jnp.exp2(m_old - m_new)
        m_pair = jnp.concatenate([m_new, m_new], axis=0)
        ps = [jnp.exp2(lg - m_pair) for lg in lgs]
        l_row = functools.reduce(jnp.add, [jnp.sum(p, axis=1, keepdims=True) for p in ps])
        pv = _dot(jnp.concatenate([p.astype(BF16) for p in ps], axis=1),
                  jnp.concatenate([side_by_side(v_refs, i) for i in blocks], axis=0))
        m_scr[...] = m_new
        l_scr[...] = alpha * l_scr[...] + l_row[:hr] + l_row[hr:]
        acc_scr[...] = alpha * acc_scr[...] + pv[:hr, :HEAD_DIM] + pv[hr:, HEAD_DIM:]

    @pl.when((phase == 1) & (g == pl.num_programs(2) - 1))
    def _():
        out = acc_scr[...] / l_scr[...]
        for h in range(heads):
            o_ref[:, head(h)] = _sigmoid(gb_ref[:, head(h)]) * out[h * rows:(h + 1) * rows, :]


def _moba_sample(proj, k_new, v_new, cache_k, cache_v, page_table, layer, rows, q_col_block, gb_col_block):
    n, d = k_new.shape
    _, _, page, heads, dh = cache_k.shape
    batch, n_pages = page_table.shape
    assert dh == HEAD_DIM and 2 * page == MOBA_BLOCK and n_pages % 2 == 0
    nb = n_pages // 2
    hr = heads * rows
    pt = page_table.reshape(-1)

    pps = 2 * math.gcd(nb, 4)
    steps = n_pages // pps

    def page_spec(step_of, i):
        return pl.BlockSpec((None, None, page, heads, dh),
                            lambda b, ph, g, pt: (layer, pt[b * n_pages + step_of(ph, g) * pps + i], 0, 0, 0))

    k_step = lambda ph, g: jnp.where(ph == 0, g, steps - 1)
    v_step = lambda ph, g: jnp.where(ph == 0, 0, g)
    row_spec = lambda cb: pl.BlockSpec((rows, d), lambda b, ph, g, pt: (b, cb))
    return pl.pallas_call(
        functools.partial(_moba_sample_kernel, heads=heads, rows=rows, page=page, pages_per_step=pps),
        grid_spec=pltpu.PrefetchScalarGridSpec(
            num_scalar_prefetch=1,
            grid=(batch, 2, steps),
            in_specs=[row_spec(q_col_block), row_spec(0), row_spec(0), row_spec(gb_col_block)]
                     + [page_spec(k_step, i) for i in range(pps)] + [page_spec(v_step, i) for i in range(pps)],
            out_specs=row_spec(0),
            scratch_shapes=[pltpu.VMEM((nb, 2 * hr, page * heads), F32),
                            pltpu.VMEM((n_pages, heads, dh), F32),
                            pltpu.VMEM((nb * heads, dh), F32),
                            pltpu.VMEM((nb, 2 * hr, dh), F32),
                            pltpu.VMEM((2 * hr, page * heads), F32),
                            pltpu.VMEM((hr, 1), F32), pltpu.VMEM((hr, 1), F32), pltpu.VMEM((hr, dh), F32)],
        ),
        out_shape=jax.ShapeDtypeStruct((n, d), F32),
        compiler_params=_cparams(("parallel", "arbitrary", "arbitrary")),
        name="moba_sample",
    )(pt, proj, k_new, v_new, proj, *([cache_k] * pps), *([cache_v] * pps))


def _merge_rows(gated_a, gated_b, x, w_out, g_post, g_pre):
    m = _dot((gated_a + gated_b).astype(BF16), w_out)
    x1 = x + _rms(m, g_post)
    return x1, _rms(x1, g_pre).astype(BF16)


def _merge_kernel(a_ref, b_ref, x_ref, w_ref, gpost_ref, gpre_ref, x1_ref, h2_ref):
    x1_ref[...], h2_ref[...] = _merge_rows(a_ref[...], b_ref[...], x_ref[...], w_ref[...],
                                           gpost_ref[...], gpre_ref[...])


def _merge(gated_a, gated_b, x, w_out, g_post, g_pre, tm):
    n, d = x.shape
    row = pl.BlockSpec((tm, d), lambda i: (i, 0))
    vec = pl.BlockSpec((1, d), lambda i: (0, 0))
    return pl.pallas_call(
        _merge_kernel,
        grid=(n // tm,),
        in_specs=[row, row, row, pl.BlockSpec((d, d), lambda i: (0, 0)), vec, vec],
        out_specs=[row, row],
        out_shape=[jax.ShapeDtypeStruct((n, d), F32), jax.ShapeDtypeStruct((n, d), BF16)],
        compiler_params=_cparams(("parallel",)),
        name="merge",
    )(gated_a, gated_b, x, w_out, g_post, g_pre)


def _gelu_tanh(x):
    return 0.5 * x * (1.0 + jnp.tanh(math.sqrt(2.0 / math.pi) * (x + 0.044715 * (x * x * x))))


def _ffn_prompt_kernel(a_ref, b_ref, x_ref, wout_ref, gpost_ref, gpre_ref, wup_ref, cw_ref, cb_ref, wdn_ref,
                       g_ref, y_ref, st_ref, carry_scr, win_scr, act_scr, *, d_ff, cs):
    tm = x_ref.shape[0]
    x1, h = _merge_rows(a_ref[...], b_ref[...], x_ref[...], wout_ref[...], gpost_ref[...], gpre_ref[...])
    halo = carry_scr.shape[0]

    @pl.when(pl.program_id(1) == 0)
    def _():
        carry_scr[...] = jnp.zeros_like(carry_scr)

    def conv(col0, win):
        cols = slice(col0, col0 + cs)
        up = _dot(h, wup_ref[:, cols])
        win[0:halo, :] = carry_scr[:, cols]
        win[halo:, :] = up
        carry_scr[:, cols] = up[tm - halo:, :]
        st_ref[0, :, cols] = up[tm - (CONV_W - 1):, :]
        cw = cw_ref[:, cols]
        return (cb_ref[:, cols] + cw[0:1, :] * win[halo - 2:halo - 2 + tm, :]
                + cw[1:2, :] * win[halo - 1:halo - 1 + tm, :] + cw[2:3, :] * up)

    for c in range(d_ff // cs):
        gate = conv(c * cs, win_scr.at[(2 * c) % win_scr.shape[0]])
        lin = conv(d_ff + c * cs, win_scr.at[(2 * c + 1) % win_scr.shape[0]])
        act_scr[:, c * cs:(c + 1) * cs] = (_gelu_tanh(gate) * lin).astype(BF16)
    y_ref[...] = x1 + _rms(_dot(act_scr[...], wdn_ref[...]), g_ref[...])


def _ffn_sample_kernel(h_ref, x_ref, wg_ref, wl_ref, cwg_ref, cwl_ref, cbg_ref, cbl_ref, wdn_ref, g_ref,
                       sg_ref, sl_ref, y_ref, ng_ref, nl_ref, acc_scr):
    c = pl.program_id(0)
    h = h_ref[...]
    batch, _, cs = sg_ref.shape
    seq_rows = h.shape[0] // batch
    t = lax.broadcasted_iota(jnp.int32, (batch, seq_rows, cs), 1)

    @pl.when(c == 0)
    def _():
        acc_scr[...] = jnp.zeros_like(acc_scr)

    def conv(w_ref, s_ref, n_ref, cw_ref, cb_ref):
        up = _dot(h, w_ref[...])
        by_seq = lambda a: a.reshape(batch, seq_rows, cs)
        n_ref[...] = by_seq(up)[:, seq_rows - (CONV_W - 1):, :]
        prev2, prev1 = s_ref[:, 0:1, :], s_ref[:, 1:2, :]
        p1 = jnp.where(t == 0, prev1, by_seq(pltpu.roll(up, 1, 0)))
        p2 = jnp.where(t == 0, prev2, jnp.where(t == 1, prev1, by_seq(pltpu.roll(up, 2, 0))))
        cw = cw_ref[...]
        return (cb_ref[...] + cw[0:1, :] * p2 + cw[1:2, :] * p1 + cw[2:3, :] * by_seq(up)).reshape(up.shape)

    gate = conv(wg_ref, sg_ref, ng_ref, cwg_ref, cbg_ref)
    lin = conv(wl_ref, sl_ref, nl_ref, cwl_ref, cbl_ref)
    acc_scr[...] += _dot((_gelu_tanh(gate) * lin).astype(BF16), wdn_ref[...])

    @pl.when(c == pl.num_programs(0) - 1)
    def _():
        y_ref[...] = x_ref[...] + _rms(acc_scr[...], g_ref[...])


def _ffn_cs(d_ff):
    for cs in (512, 256, 128):
        if d_ff % cs == 0:
            return cs
    raise ValueError(f"d_ff={d_ff} is not a multiple of the 128-lane tile")


def _ffn_prompt(gated_a, gated_b, x, w_out, g_post, g_pre, w_up, conv_w, conv_b, w_down, gain, batch, seq, tm):
    n, d = x.shape
    d_ff = w_down.shape[0]
    n_t = seq // tm
    full = lambda a: pl.BlockSpec(a.shape, lambda b, t: (0, 0), pipeline_mode=pl.Buffered(1))
    row = pl.BlockSpec((tm, d), lambda b, t: (b * n_t + t, 0))
    cs = _ffn_cs(d_ff)
    halo = 8
    return pl.pallas_call(
        functools.partial(_ffn_prompt_kernel, d_ff=d_ff, cs=cs),
        grid=(batch, n_t),
        in_specs=[row, row, row, full(w_out), full(g_post), full(g_pre),
                  full(w_up), full(conv_w), full(conv_b), full(w_down), full(gain)],
        out_specs=[row, pl.BlockSpec((1, CONV_W - 1, 2 * d_ff), lambda b, t: (b, 0, 0))],
        out_shape=[jax.ShapeDtypeStruct((n, d), F32),
                   jax.ShapeDtypeStruct((batch, CONV_W - 1, 2 * d_ff), F32)],
        scratch_shapes=[pltpu.VMEM((halo, 2 * d_ff), F32), pltpu.VMEM((4, halo + tm, cs), F32),
                        pltpu.VMEM((tm, d_ff), BF16)],
        compiler_params=_cparams(("parallel", "arbitrary")),
        name="ffn_prompt",
    )(gated_a, gated_b, x, w_out, g_post, g_pre, w_up, conv_w, conv_b, w_down, gain)


def _ffn_sample(h2, x1, w_up, conv_w, conv_b, w_down, gain, state, seq):
    n, d = x1.shape
    d_ff = w_down.shape[0]
    batch = n // seq
    cs = _ffn_cs(d_ff)
    nc = d_ff // cs
    full = lambda a: pl.BlockSpec(a.shape, lambda c: (0, 0))
    gate = lambda rows: pl.BlockSpec((rows, cs), lambda c: (0, c))
    lin = lambda rows: pl.BlockSpec((rows, cs), lambda c: (0, nc + c))
    st_gate = pl.BlockSpec((batch, CONV_W - 1, cs), lambda c: (0, 0, c))
    st_lin = pl.BlockSpec((batch, CONV_W - 1, cs), lambda c: (0, 0, nc + c))
    half_state = jax.ShapeDtypeStruct((batch, CONV_W - 1, d_ff), F32)
    y, new_gate, new_lin = pl.pallas_call(
        _ffn_sample_kernel,
        grid=(nc,),
        in_specs=[full(h2), full(x1), gate(d), lin(d), gate(CONV_W), lin(CONV_W), gate(1), lin(1),
                  pl.BlockSpec((cs, d), lambda c: (c, 0)), full(gain), st_gate, st_lin],
        out_specs=[full(x1), st_gate, st_gate],
        out_shape=[jax.ShapeDtypeStruct((n, d), F32), half_state, half_state],
        scratch_shapes=[pltpu.VMEM((n, d), F32)],
        compiler_params=_cparams(("arbitrary",)),
        name="ffn_sample",
    )(h2, x1, w_up, w_up, conv_w, conv_w, conv_b, conv_b, w_down, gain, state, state)
    return y, jnp.concatenate([new_gate, new_lin], axis=-1)


def _row_tile(n, cap):
    t = min(n, cap)
    while n % t:
        t //= 2
    return t


def kernel(x_prompt, x_sample, cache_k, cache_v, page_table, state_hgrn, state_ffn_conv, norm_mix_pre, norm_mix_post, w_in, hgrn_lb_logits, hgrn_norm, w_out, norm_ffn_pre, norm_ffn_post, w_up, conv_w, conv_b, w_down):
    bp, tp, d = x_prompt.shape
    bs, ts, _ = x_sample.shape
    depth = w_in.shape[0]
    heads = d // HEAD_DIM
    lb_all = jnp.cumsum(jax.nn.softmax(hgrn_lb_logits.astype(F32), axis=0), axis=0)

    xp = x_prompt.reshape(bp * tp, d)
    xs = x_sample.reshape(bs * ts, d)
    outs = [[] for _ in range(8)]
    for l in range(depth):
        w_in_b = w_in[l].astype(BF16)
        w_out_b = w_out[l].astype(BF16)
        w_up_b = w_up[l].astype(BF16)
        w_down_b = w_down[l].astype(BF16)
        vec = lambda a: a[l].reshape(1, -1)
        lb = lb_all[l].reshape(1, d)
        Q_BLK, GA_BLK, GB_BLK = 4, 5, 6

        def mixers(x, batch, seq, s0, moba):
            proj, k_new, v_new = _inproj(x, vec(norm_mix_pre), w_in_b, _row_tile(x.shape[0], 1024), d)
            gated_a, s_new = _hgrn(proj, lb, vec(hgrn_norm), s0, batch, seq, _row_tile(seq, 256), GA_BLK)
            return gated_a, moba(proj, k_new, v_new), k_new, v_new, s_new

        s0p = jnp.zeros((bp, heads, HEAD_DIM, HEAD_DIM), F32)
        ga, gb, kp, vp, sp = mixers(xp, bp, tp, s0p,
                                    lambda pr, k, v: _moba_prompt(pr, k, v, bp, tp, Q_BLK, GB_BLK))
        xp, cp = _ffn_prompt(ga, gb, xp, w_out_b, vec(norm_mix_post), vec(norm_ffn_pre), w_up_b, conv_w[l],
                             conv_b[l].reshape(1, -1), w_down_b, vec(norm_ffn_post), bp, tp, _row_tile(tp, 512))
        ga, gb, kn, vn, sn = mixers(xs, bs, ts, state_hgrn[l],
                                    lambda pr, k, v: _moba_sample(pr, k, v, cache_k, cache_v, page_table, l, ts,
                                                                  Q_BLK, GB_BLK))
        x1, h2 = _merge(ga, gb, xs, w_out_b, vec(norm_mix_post), vec(norm_ffn_pre), _row_tile(bs * ts, 512))
        xs, cn = _ffn_sample(h2, x1, w_up_b, conv_w[l], conv_b[l].reshape(1, -1), w_down_b,
                             vec(norm_ffn_post), state_ffn_conv[l], ts)
        kv = lambda a, b, t: a.reshape(b, t, heads, HEAD_DIM)
        for o, val in zip(outs, (kv(kp, bp, tp), kv(vp, bp, tp), sp, cp, kv(kn, bs, ts), kv(vn, bs, ts), sn, cn)):
            o.append(val)
    return (xp.reshape(bp, tp, d), xs.reshape(bs, ts, d), *(jnp.stack(o) for o in outs))
```

```python
import functools
import math

import jax
import jax.numpy as jnp
from jax import lax
from jax.experimental import pallas as pl
from jax.experimental.pallas import tpu as pltpu

F32 = jnp.float32
BF16 = jnp.bfloat16

NORM_EPS = 1e-6
HEAD_DIM = 128
MOBA_BLOCK = 256
MOBA_TOPK = 3
HGRN_CHUNK = 64
HGRN_SEQS_PER_STEP = 4
CONV_W = 3
BF16_ROWS = 16
SOFTMAX_LOG2_SCALE = math.log2(math.e) / math.sqrt(HEAD_DIM)
VMEM_LIMIT = 56 * 1024 * 1024


def _cparams(sem):
    return pltpu.CompilerParams(dimension_semantics=sem, vmem_limit_bytes=VMEM_LIMIT)


def _sigmoid(x):
    return 0.5 * jnp.tanh(0.5 * x) + 0.5


def _rms(x, gain):
    return x * lax.rsqrt(jnp.mean(x * x, axis=-1, keepdims=True) + NORM_EPS) * gain


def _dot(a, b):
    return jnp.dot(a, b, preferred_element_type=F32)


def _dot_nt(a, b):
    return lax.dot_general(a, b, (((1,), (1,)), ((), ())), preferred_element_type=F32)


def _split_bf16(x, parts):
    out = []
    for _ in range(parts):
        p = x.astype(BF16)
        out.append(p)
        x = x - p.astype(F32)
    return out


def _dot_nt_split(a, b, operand_dtype=BF16):
    ah, al = (p.astype(operand_dtype) for p in _split_bf16(a, 2))
    bh, bl = (p.astype(operand_dtype) for p in _split_bf16(b, 2))
    return _dot_nt(ah, bh) + _dot_nt(ah, bl) + _dot_nt(al, bh)


def _inproj_kernel(x_ref, g_ref, w_ref, p_ref, k_ref, v_ref, h_scr, *, main_steps, kv_steps):
    j = pl.program_id(1)

    @pl.when(j == 0)
    def _():
        h_scr[...] = _rms(x_ref[...], g_ref[...]).astype(BF16)

    @pl.when(j < main_steps)
    def _():
        p_ref[...] = _dot(h_scr[...], w_ref[...])

    @pl.when((j >= main_steps) & (j < main_steps + kv_steps))
    def _():
        k_ref[...] = _dot(h_scr[...], w_ref[...])

    @pl.when(j >= main_steps + kv_steps)
    def _():
        v_ref[...] = _dot(h_scr[...], w_ref[...])


def _inproj(x, gain, w_in, tm, tn):
    n, d = x.shape
    per = d // tn
    main_steps, kv_steps = 7 * per, per
    src_block = lambda jb: jnp.where(jb < 5, jb, jnp.where(jb < 7, jb + 2, jb - 2))
    return pl.pallas_call(
        functools.partial(_inproj_kernel, main_steps=main_steps, kv_steps=kv_steps),
        grid=(n // tm, 9 * per),
        in_specs=[
            pl.BlockSpec((tm, d), lambda i, j: (i, 0)),
            pl.BlockSpec((1, d), lambda i, j: (0, 0)),
            pl.BlockSpec((d, tn), lambda i, j: (0, src_block(j // per) * per + j % per)),
        ],
        out_specs=[
            pl.BlockSpec((tm, tn), lambda i, j: (i, jnp.minimum(j, main_steps - 1))),
            pl.BlockSpec((tm, tn), lambda i, j: (i, jnp.clip(j - main_steps, 0, kv_steps - 1))),
            pl.BlockSpec((tm, tn), lambda i, j: (i, jnp.clip(j - main_steps - kv_steps, 0, kv_steps - 1))),
        ],
        out_shape=[
            jax.ShapeDtypeStruct((n, 7 * d), F32),
            jax.ShapeDtypeStruct((n, d), F32),
            jax.ShapeDtypeStruct((n, d), F32),
        ],
        scratch_shapes=[pltpu.VMEM((tm, d), BF16)],
        compiler_params=_cparams(("parallel", "arbitrary")),
        name="inproj",
    )(x, gain, w_in)


def _hgrn_kernel(hq_ref, hf_ref, hi_ref, hg_ref, ga_ref, lb_ref, gain_ref, s0_ref, a_ref, sout_ref, st_scr,
                 *, heads, seqs):
    tt = pl.program_id(1)
    C = HGRN_CHUNK
    assert 2 * C == HEAD_DIM
    rows = hq_ref.shape[0] // seqs
    d = heads * HEAD_DIM
    n_chunks = -(-rows // C)
    pairs = [(s, h) for s in range(seqs) for h in range(heads)]
    lanes = lambda h: slice(h * HEAD_DIM, (h + 1) * HEAD_DIM)

    @pl.when(tt == 0)
    def _():
        for s, h in pairs:
            st_scr[s * heads + h] = s0_ref[s, h].T

    lb = lb_ref[...]
    gain = gain_ref[...]
    tril = jnp.where(lax.broadcasted_iota(jnp.int32, (C, C), 0) >= lax.broadcasted_iota(jnp.int32, (C, C), 1),
                     1.0, 0.0).astype(BF16)
    causal = (lax.broadcasted_iota(jnp.int32, (C, HEAD_DIM), 1) <= lax.broadcasted_iota(jnp.int32, (C, HEAD_DIM), 0))
    zero_rows = jnp.zeros((C, HEAD_DIM), BF16)
    states = {(s, h): st_scr[s * heads + h] for s, h in pairs}

    def chunk_operands(s, c):
        live = min(rows, (c + 1) * C) - c * C
        rs = slice(s * rows + c * C, s * rows + c * C + live)

        def padded(x, fill):
            return jnp.concatenate([x, jnp.full((C - live, d), fill, F32)], axis=0) if live < C else x

        f = padded(lb + (1.0 - lb) * _sigmoid(hf_ref[rs, :]), 1.0)
        hq = hq_ref[rs, :]
        q = padded(hq * _sigmoid(hq), 0.0)
        vc = padded(hi_ref[rs, :], 0.0)
        b = functools.reduce(jnp.add, [_dot(tril, part) for part in _split_bf16(jnp.log(f), 3)])
        decay = jnp.exp(b[C - 1:C])
        k_scaled = (1.0 - f) * jnp.exp(-b)
        qt = (q * jnp.exp(b)).astype(BF16)
        kh = (k_scaled * decay).astype(BF16)
        return rs, live, decay, qt, k_scaled.astype(BF16), kh, vc, vc.astype(BF16)

    ahead = [chunk_operands(s, 0) for s in range(seqs)]
    for c in range(n_chunks):
        cur = ahead
        if c + 1 < n_chunks:
            ahead = [chunk_operands(s, c + 1) for s in range(seqs)]
        decay, qt, kt, kh, vc, vb = ({s: cur[s][i] for s in range(seqs)} for i in range(2, 8))
        out1 = {(s, h): _dot_nt(qt[s][:, lanes(h)], jnp.concatenate(
            [states[s, h].astype(BF16), kt[s][:, lanes(h)], zero_rows], axis=0)) for s, h in pairs}
        v_t = {(s, h): jnp.concatenate([jnp.zeros((C, HEAD_DIM), F32), vc[s][:, lanes(h)]], axis=0).T.astype(BF16)
               for s, h in pairs}
        scores = {p: jnp.where(causal, out1[p][:, HEAD_DIM:], 0.0).astype(BF16) for p in pairs}
        out2 = {(s, h): _dot(jnp.concatenate([scores[s, h], v_t[s, h]], axis=0),
                             jnp.concatenate([vb[s][:, lanes(h)], kh[s][:, lanes(h)]], axis=0)) for s, h in pairs}
        states = {(s, h): states[s, h] * decay[s][:, lanes(h)] + out2[s, h][C:] for s, h in pairs}
        for s in range(seqs):
            rs, live = cur[s][:2]
            normed = jnp.concatenate([_rms(out1[s, h][:, :HEAD_DIM] + out2[s, h][:C], gain[:, lanes(h)])
                                      for h in range(heads)], axis=1)
            hg = hg_ref[rs, :]
            a_ref[rs, :] = _sigmoid(ga_ref[rs, :]) * (normed[:live] * (hg * _sigmoid(hg)))

    for s, h in pairs:
        st_scr[s * heads + h] = states[s, h]

    @pl.when(tt == pl.num_programs(1) - 1)
    def _():
        for s, h in pairs:
            sout_ref[s, h] = st_scr[s * heads + h].T


def _hgrn(proj, lb, gain, s0, batch, seq, t_tile, ga_block):
    n = proj.shape[0]
    d = lb.shape[1]
    heads = d // HEAD_DIM
    n_t = seq // t_tile
    seqs = math.gcd(batch, HGRN_SEQS_PER_STEP) if n_t == 1 and t_tile < HGRN_CHUNK else 1
    col_spec = lambda cb: pl.BlockSpec((seqs * t_tile, d), lambda b, t: (b * n_t + t, cb))
    state_spec = pl.BlockSpec((seqs, heads, HEAD_DIM, HEAD_DIM), lambda b, t: (b, 0, 0, 0))
    return pl.pallas_call(
        functools.partial(_hgrn_kernel, heads=heads, seqs=seqs),
        grid=(batch // seqs, n_t),
        in_specs=[col_spec(0), col_spec(1), col_spec(2), col_spec(3), col_spec(ga_block),
                  pl.BlockSpec((1, d), lambda b, t: (0, 0)),
                  pl.BlockSpec((1, d), lambda b, t: (0, 0)),
                  state_spec],
        out_specs=[col_spec(0), state_spec],
        out_shape=[jax.ShapeDtypeStruct((n, d), F32),
                   jax.ShapeDtypeStruct((batch, heads, HEAD_DIM, HEAD_DIM), F32)],
        scratch_shapes=[pltpu.VMEM((seqs * heads, HEAD_DIM, HEAD_DIM), F32)],
        compiler_params=_cparams(("parallel", "arbitrary")),
        name="hgrn",
    )(proj, proj, proj, proj, proj, lb, gain, s0)


def _topk_bias(s, valid, topk):
    n_blocks = len(valid)
    rows = []
    for n in range(n_blocks):
        sn = s[n:n + 1, :]
        cnt = jnp.zeros(sn.shape, jnp.int32)
        for m in range(n_blocks):
            if m == n:
                continue
            sm = s[m:m + 1, :]
            beats = (sm >= sn) if m < n else (sm > sn)
            cnt = cnt + jnp.where(beats & valid[m], 1, 0)
        rows.append(jnp.where(cnt < topk, 0.0, -jnp.inf))
    return rows


def _moba_prompt_kernel(q_ref, k_ref, v_ref, gb_ref, o_ref, *, nb, blk):
    t = nb * blk
    q = q_ref[...]
    k = k_ref[...]
    qb = (q * SOFTMAX_LOG2_SCALE).astype(BF16)
    kb = k.astype(BF16)
    vt = v_ref[...].T.astype(BF16)

    kmean = jnp.concatenate(
        [jnp.mean(k[n * blk:(n + 1) * blk, :], axis=0, keepdims=True) for n in range(nb)]
        + [jnp.zeros((BF16_ROWS - nb % BF16_ROWS, HEAD_DIM), F32)], axis=0)
    s = _dot_nt_split(kmean, q)
    q_blk = lax.broadcasted_iota(jnp.int32, (1, t), 1) // blk
    bias = _topk_bias(s, [q_blk > m for m in range(nb - 1)], MOBA_TOPK)

    key = lax.broadcasted_iota(jnp.int32, (blk, blk), 0)
    qry = lax.broadcasted_iota(jnp.int32, (blk, blk), 1)
    logits = lambda j: _dot_nt(kb[:(j + 1) * blk, :], qb[j * blk:(j + 1) * blk, :])
    lg_next = logits(nb - 1)
    for j in reversed(range(nb)):
        qs = slice(j * blk, (j + 1) * blk)
        lg = lg_next
        if j > 0:
            lg_next = logits(j - 1)
        parts = [lg[n * blk:(n + 1) * blk, :] + bias[n][:, qs] for n in range(j)]
        parts.append(jnp.where(key <= qry, lg[j * blk:, :], -jnp.inf))
        m = functools.reduce(jnp.maximum, [jnp.max(p, axis=0, keepdims=True) for p in parts])
        ps = [jnp.exp2(p - m) for p in parts]
        l = functools.reduce(jnp.add, [jnp.sum(p, axis=0, keepdims=True) for p in ps])
        pb = jnp.concatenate([p.astype(BF16) for p in ps], axis=0)
        o_ref[qs, :] = _sigmoid(gb_ref[qs, :]) * (_dot(vt[:, :(j + 1) * blk], pb) / l).T


def _moba_prompt(proj, k_new, v_new, batch, seq, q_col_block, gb_col_block):
    n, d = k_new.shape
    heads = d // HEAD_DIM
    blk = MOBA_BLOCK
    nb = seq // blk
    seq_head = lambda cb: pl.BlockSpec((seq, HEAD_DIM), lambda b, h: (b, cb * heads + h))
    return pl.pallas_call(
        functools.partial(_moba_prompt_kernel, nb=nb, blk=blk),
        grid=(batch, heads),
        in_specs=[seq_head(q_col_block), seq_head(0), seq_head(0), seq_head(gb_col_block)],
        out_specs=seq_head(0),
        out_shape=jax.ShapeDtypeStruct((n, d), F32),
        compiler_params=_cparams(("parallel", "parallel")),
        name="moba_prompt",
    )(proj, k_new, v_new, proj)


def _block_diag_queries(q, heads, rows):
    d = q.shape[1]
    tiled = jnp.concatenate([q] * heads, axis=0)
    r = lax.broadcasted_iota(jnp.int32, (heads * rows, d), 0) // rows
    c = lax.broadcasted_iota(jnp.int32, (heads * rows, d), 1) // HEAD_DIM
    return jnp.where(r == c, tiled, 0.0)


def _moba_sample_kernel(pt_ref, q_ref, kn_ref, vn_ref, gb_ref, *refs, heads, rows, page, pages_per_step):
    del pt_ref
    k_refs, v_refs = refs[:pages_per_step], refs[pages_per_step:2 * pages_per_step]
    o_ref, lg_scr, ksum_scr, kmean_scr, bias_scr, hmask_scr, m_scr, l_scr, acc_scr = refs[2 * pages_per_step:]
    phase, g = pl.program_id(1), pl.program_id(2)
    blocks = range(pages_per_step // 2)
    hr = heads * rows
    d = heads * HEAD_DIM
    nb = bias_scr.shape[0]
    head = lambda h: slice(h * HEAD_DIM, (h + 1) * HEAD_DIM)
    pos_head = lambda ref: ref[...].reshape(page * heads, HEAD_DIM).astype(BF16)
    side_by_side = lambda refs, i: jnp.concatenate([pos_head(refs[2 * i]), pos_head(refs[2 * i + 1])], axis=1)

    q = q_ref[...]
    qs = q * SOFTMAX_LOG2_SCALE

    @pl.when(phase == 0)
    def _():
        for i, k_ref in enumerate(k_refs):
            ksum_scr[g * pages_per_step + i] = jnp.sum(k_ref[...], axis=0)
        q_hd = jnp.concatenate([qs[:, head(h)] for h in range(heads)], axis=0).astype(BF16)
        zero = jnp.zeros_like(q_hd)
        q_pair = jnp.concatenate([jnp.concatenate([q_hd, zero], axis=1),
                                  jnp.concatenate([zero, q_hd], axis=1)], axis=0)
        for i in blocks:
            lg_scr[g * len(blocks) + i] = _dot_nt(q_pair, side_by_side(k_refs, i))

    @pl.when((phase == 1) & (g == 0))
    def _():
        for n in range(nb):
            kmean_scr[n * heads:(n + 1) * heads, :] = (ksum_scr[2 * n] + ksum_scr[2 * n + 1]) * (0.5 / page)
        s = jnp.concatenate([_dot_nt_split(q[:, head(h)], kmean_scr[pl.ds(h, nb, stride=heads), :], F32)
                             for h in range(heads)], axis=0)
        cnt = jnp.zeros((hr, nb), jnp.int32)
        idx = lax.broadcasted_iota(jnp.int32, (hr, nb), 1)
        for m in range(nb):
            sm = s[:, m:m + 1]
            beats = (sm > s) | ((sm == s) & (idx > m))
            cnt = cnt + jnp.where(beats, 1, 0)
        bias = jnp.where(cnt < MOBA_TOPK, 0.0, -jnp.inf)
        for n in range(nb):
            col = jnp.broadcast_to(bias[:, n:n + 1], (hr, HEAD_DIM))
            bias_scr[n] = jnp.concatenate([col, col], axis=0)
        r = lax.broadcasted_iota(jnp.int32, hmask_scr.shape, 0)
        c = lax.broadcasted_iota(jnp.int32, hmask_scr.shape, 1)
        hmask_scr[...] = jnp.where(c % heads == (r % hr) // rows, 0.0, -jnp.inf)

        qbd = _block_diag_queries(qs, heads, rows).astype(BF16)
        kn = jnp.concatenate([kn_ref[...], jnp.zeros((page - rows, d), F32)], axis=0)
        vn = jnp.concatenate([vn_ref[...], jnp.zeros((page - rows, d), F32)], axis=0)
        r_own = lax.broadcasted_iota(jnp.int32, (hr, page), 0) % rows
        c_own = lax.broadcasted_iota(jnp.int32, (hr, page), 1)
        lg_own = jnp.where(c_own <= r_own, _dot_nt(qbd, kn.astype(BF16)), -jnp.inf)
        m0 = jnp.max(lg_own, axis=1, keepdims=True)
        p_own = jnp.exp2(lg_own - m0)
        pv = _dot(p_own.astype(BF16), vn.astype(BF16))
        m_scr[...] = m0
        l_scr[...] = jnp.sum(p_own, axis=1, keepdims=True)
        acc_scr[...] = jnp.concatenate([pv[h * rows:(h + 1) * rows, head(h)] for h in range(heads)], axis=0)

    @pl.when(phase == 1)
    def _():
        hmask = hmask_scr[...]
        lgs = [lg_scr[g * len(blocks) + i] + hmask
               + jnp.concatenate([bias_scr[g * len(blocks) + i]] * heads, axis=1) for i in blocks]
        m_row = functools.reduce(jnp.maximum, [jnp.max(lg, axis=1, keepdims=True) for lg in lgs])
        m_old = m_scr[...]
        m_new = jnp.maximum(m_old, jnp.maximum(m_row[:hr], m_row[hr:]))
        alpha = jnp.exp2(m_old - m_new)
        m_pair = jnp.concatenate([m_new, m_new], axis=0)
        ps = [jnp.exp2(lg - m_pair) for lg in lgs]
        l_row = functools.reduce(jnp.add, [jnp.sum(p, axis=1, keepdims=True) for p in ps])
        pv = _dot(jnp.concatenate([p.astype(BF16) for p in ps], axis=1),
                  jnp.concatenate([side_by_side(v_refs, i) for i in blocks], axis=0))
        m_scr[...] = m_new
        l_scr[...] = alpha * l_scr[...] + l_row[:hr] + l_row[hr:]
        acc_scr[...] = alpha * acc_scr[...] + pv[:hr, :HEAD_DIM] + pv[hr:, HEAD_DIM:]

    @pl.when((phase == 1) & (g == pl.num_programs(2) - 1))
    def _():
        out = acc_scr[...] / l_scr[...]
        for h in range(heads):
            o_ref[:, head(h)] = _sigmoid(gb_ref[:, head(h)]) * out[h * rows:(h + 1) * rows, :]


def _moba_sample(proj, k_new, v_new, cache_k, cache_v, page_table, layer, rows, q_col_block, gb_col_block):
    n, d = k_new.shape
    _, _, page, heads, dh = cache_k.shape
    batch, n_pages = page_table.shape
    assert dh == HEAD_DIM and 2 * page == MOBA_BLOCK and n_pages % 2 == 0
    nb = n_pages // 2
    hr = heads * rows
    pt = page_table.reshape(-1)

    pps = 2 * math.gcd(nb, 4)
    steps = n_pages // pps

    def page_spec(step_of, i):
        return pl.BlockSpec((None, None, page, heads, dh),
                            lambda b, ph, g, pt: (layer, pt[b * n_pages + step_of(ph, g) * pps + i], 0, 0, 0))

    k_step = lambda ph, g: jnp.where(ph == 0, g, steps - 1)
    v_step = lambda ph, g: jnp.where(ph == 0, 0, g)
    row_spec = lambda cb: pl.BlockSpec((rows, d), lambda b, ph, g, pt: (b, cb))
    return pl.pallas_call(
        functools.partial(_moba_sample_kernel, heads=heads, rows=rows, page=page, pages_per_step=pps),
        grid_spec=pltpu.PrefetchScalarGridSpec(
            num_scalar_prefetch=1,
            grid=(batch, 2, steps),
            in_specs=[row_spec(q_col_block), row_spec(0), row_spec(0), row_spec(gb_col_block)]
                     + [page_spec(k_step, i) for i in range(pps)] + [page_spec(v_step, i) for i in range(pps)],
            out_specs=row_spec(0),
            scratch_shapes=[pltpu.VMEM((nb, 2 * hr, page * heads), F32),
                            pltpu.VMEM((n_pages, heads, dh), F32),
                            pltpu.VMEM((nb * heads, dh), F32),
                            pltpu.VMEM((nb, 2 * hr, dh), F32),
                            pltpu.VMEM((2 * hr, page * heads), F32),
                            pltpu.VMEM((hr, 1), F32), pltpu.VMEM((hr, 1), F32), pltpu.VMEM((hr, dh), F32)],
        ),
        out_shape=jax.ShapeDtypeStruct((n, d), F32),
        compiler_params=_cparams(("parallel", "arbitrary", "arbitrary")),
        name="moba_sample",
    )(pt, proj, k_new, v_new, proj, *([cache_k] * pps), *([cache_v] * pps))


def _merge_rows(gated_a, gated_b, x, w_out, g_post, g_pre):
    m = _dot((gated_a + gated_b).astype(BF16), w_out)
    x1 = x + _rms(m, g_post)
    return x1, _rms(x1, g_pre).astype(BF16)


def _merge_kernel(a_ref, b_ref, x_ref, w_ref, gpost_ref, gpre_ref, x1_ref, h2_ref):
    x1_ref[...], h2_ref[...] = _merge_rows(a_ref[...], b_ref[...], x_ref[...], w_ref[...],
                                           gpost_ref[...], gpre_ref[...])


def _merge(gated_a, gated_b, x, w_out, g_post, g_pre, tm):
    n, d = x.shape
    row = pl.BlockSpec((tm, d), lambda i: (i, 0))
    vec = pl.BlockSpec((1, d), lambda i: (0, 0))
    return pl.pallas_call(
        _merge_kernel,
        grid=(n // tm,),
        in_specs=[row, row, row, pl.BlockSpec((d, d), lambda i: (0, 0)), vec, vec],
        out_specs=[row, row],
        out_shape=[jax.ShapeDtypeStruct((n, d), F32), jax.ShapeDtypeStruct((n, d), BF16)],
        compiler_params=_cparams(("parallel",)),
        name="merge",
    )(gated_a, gated_b, x, w_out, g_post, g_pre)


def _gelu_tanh(x):
    return 0.5 * x * (1.0 + jnp.tanh(math.sqrt(2.0 / math.pi) * (x + 0.044715 * (x * x * x))))


def _ffn_prompt_kernel(a_ref, b_ref, x_ref, wout_ref, gpost_ref, gpre_ref, wup_ref, cw_ref, cb_ref, wdn_ref,
                       g_ref, y_ref, st_ref, carry_scr, win_scr, act_scr, *, d_ff, cs):
    tm = x_ref.shape[0]
    x1, h = _merge_rows(a_ref[...], b_ref[...], x_ref[...], wout_ref[...], gpost_ref[...], gpre_ref[...])
    halo = carry_scr.shape[0]

    @pl.when(pl.program_id(1) == 0)
    def _():
        carry_scr[...] = jnp.zeros_like(carry_scr)

    def conv(col0, win):
        cols = slice(col0, col0 + cs)
        up = _dot(h, wup_ref[:, cols])
        win[0:halo, :] = carry_scr[:, cols]
        win[halo:, :] = up
        carry_scr[:, cols] = up[tm - halo:, :]
        st_ref[0, :, cols] = up[tm - (CONV_W - 1):, :]
        cw = cw_ref[:, cols]
        return (cb_ref[:, cols] + cw[0:1, :] * win[halo - 2:halo - 2 + tm, :]
                + cw[1:2, :] * win[halo - 1:halo - 1 + tm, :] + cw[2:3, :] * up)

    for c in range(d_ff // cs):
        gate = conv(c * cs, win_scr.at[(2 * c) % win_scr.shape[0]])
        lin = conv(d_ff + c * cs, win_scr.at[(2 * c + 1) % win_scr.shape[0]])
        act_scr[:, c * cs:(c + 1) * cs] = (_gelu_tanh(gate) * lin).astype(BF16)
    y_ref[...] = x1 + _rms(_dot(act_scr[...], wdn_ref[...]), g_ref[...])


def _ffn_sample_kernel(h_ref, x_ref, wg_ref, wl_ref, cwg_ref, cwl_ref, cbg_ref, cbl_ref, wdn_ref, g_ref,
                       sg_ref, sl_ref, y_ref, ng_ref, nl_ref, acc_scr):
    c = pl.program_id(0)
    h = h_ref[...]
    batch, _, cs = sg_ref.shape
    seq_rows = h.shape[0] // batch
    t = lax.broadcasted_iota(jnp.int32, (batch, seq_rows, cs), 1)

    @pl.when(c == 0)
    def _():
        acc_scr[...] = jnp.zeros_like(acc_scr)

    def conv(w_ref, s_ref, n_ref, cw_ref, cb_ref):
        up = _dot(h, w_ref[...])
        by_seq = lambda a: a.reshape(batch, seq_rows, cs)
        n_ref[...] = by_seq(up)[:, seq_rows - (CONV_W - 1):, :]
        prev2, prev1 = s_ref[:, 0:1, :], s_ref[:, 1:2, :]
        p1 = jnp.where(t == 0, prev1, by_seq(pltpu.roll(up, 1, 0)))
        p2 = jnp.where(t == 0, prev2, jnp.where(t == 1, prev1, by_seq(pltpu.roll(up, 2, 0))))
        cw = cw_ref[...]
        return (cb_ref[...] + cw[0:1, :] * p2 + cw[1:2, :] * p1 + cw[2:3, :] * by_seq(up)).reshape(up.shape)

    gate = conv(wg_ref, sg_ref, ng_ref, cwg_ref, cbg_ref)
    lin = conv(wl_ref, sl_ref, nl_ref, cwl_ref, cbl_ref)
    acc_scr[...] += _dot((_gelu_tanh(gate) * lin).astype(BF16), wdn_ref[...])

    @pl.when(c == pl.num_programs(0) - 1)
    def _():
        y_ref[...] = x_ref[...] + _rms(acc_scr[...], g_ref[...])


def _ffn_cs(d_ff):
    for cs in (512, 256, 128):
        if d_ff % cs == 0:
            return cs
    raise ValueError(f"d_ff={d_ff} is not a multiple of the 128-lane tile")


def _ffn_prompt(gated_a, gated_b, x, w_out, g_post, g_pre, w_up, conv_w, conv_b, w_down, gain, batch, seq, tm):
    n, d = x.shape
    d_ff = w_down.shape[0]
    n_t = seq // tm
    full = lambda a: pl.BlockSpec(a.shape, lambda b, t: (0, 0), pipeline_mode=pl.Buffered(1))
    row = pl.BlockSpec((tm, d), lambda b, t: (b * n_t + t, 0))
    cs = _ffn_cs(d_ff)
    halo = 8
    return pl.pallas_call(
        functools.partial(_ffn_prompt_kernel, d_ff=d_ff, cs=cs),
        grid=(batch, n_t),
        in_specs=[row, row, row, full(w_out), full(g_post), full(g_pre),
                  full(w_up), full(conv_w), full(conv_b), full(w_down), full(gain)],
        out_specs=[row, pl.BlockSpec((1, CONV_W - 1, 2 * d_ff), lambda b, t: (b, 0, 0))],
        out_shape=[jax.ShapeDtypeStruct((n, d), F32),
                   jax.ShapeDtypeStruct((batch, CONV_W - 1, 2 * d_ff), F32)],
        scratch_shapes=[pltpu.VMEM((halo, 2 * d_ff), F32), pltpu.VMEM((4, halo + tm, cs), F32),
                        pltpu.VMEM((tm, d_ff), BF16)],
        compiler_params=_cparams(("parallel", "arbitrary")),
        name="ffn_prompt",
    )(gated_a, gated_b, x, w_out, g_post, g_pre, w_up, conv_w, conv_b, w_down, gain)


def _ffn_sample(h2, x1, w_up, conv_w, conv_b, w_down, gain, state, seq):
    n, d = x1.shape
    d_ff = w_down.shape[0]
    batch = n // seq
    cs = _ffn_cs(d_ff)
    nc = d_ff // cs
    full = lambda a: pl.BlockSpec(a.shape, lambda c: (0, 0))
    gate = lambda rows: pl.BlockSpec((rows, cs), lambda c: (0, c))
    lin = lambda rows: pl.BlockSpec((rows, cs), lambda c: (0, nc + c))
    st_gate = pl.BlockSpec((batch, CONV_W - 1, cs), lambda c: (0, 0, c))
    st_lin = pl.BlockSpec((batch, CONV_W - 1, cs), lambda c: (0, 0, nc + c))
    half_state = jax.ShapeDtypeStruct((batch, CONV_W - 1, d_ff), F32)
    y, new_gate, new_lin = pl.pallas_call(
        _ffn_sample_kernel,
        grid=(nc,),
        in_specs=[full(h2), full(x1), gate(d), lin(d), gate(CONV_W), lin(CONV_W), gate(1), lin(1),
                  pl.BlockSpec((cs, d), lambda c: (c, 0)), full(gain), st_gate, st_lin],
        out_specs=[full(x1), st_gate, st_gate],
        out_shape=[jax.ShapeDtypeStruct((n, d), F32), half_state, half_state],
        scratch_shapes=[pltpu.VMEM((n, d), F32)],
        compiler_params=_cparams(("arbitrary",)),
        name="ffn_sample",
    )(h2, x1, w_up, w_up, conv_w, conv_w, conv_b, conv_b, w_down, gain, state, state)
    return y, jnp.concatenate([new_gate, new_lin], axis=-1)


def _row_tile(n, cap):
    t = min(n, cap)
    while n % t:
        t //= 2
    return t


def kernel(x_prompt, x_sample, cache_k, cache_v, page_table, state_hgrn, state_ffn_conv, norm_mix_pre, norm_mix_post, w_in, hgrn_lb_logits, hgrn_norm, w_out, norm_ffn_pre, norm_ffn_post, w_up, conv_w, conv_b, w_down):
    bp, tp, d = x_prompt.shape
    bs, ts, _ = x_sample.shape
    depth = w_in.shape[0]
    heads = d // HEAD_DIM
    lb_all = jnp.cumsum(jax.nn.softmax(hgrn_lb_logits.astype(F32), axis=0), axis=0)

    xp = x_prompt.reshape(bp * tp, d)
    xs = x_sample.reshape(bs * ts, d)
    outs = [[] for _ in range(8)]
    for l in range(depth):
        w_in_b = w_in[l].astype(BF16)
        w_out_b = w_out[l].astype(BF16)
        w_up_b = w_up[l].astype(BF16)
        w_down_b = w_down[l].astype(BF16)
        vec = lambda a: a[l].reshape(1, -1)
        lb = lb_all[l].reshape(1, d)
        Q_BLK, GA_BLK, GB_BLK = 4, 5, 6

        def mixers(x, batch, seq, s0, moba):
            proj, k_new, v_new = _inproj(x, vec(norm_mix_pre), w_in_b, _row_tile(x.shape[0], 1024), d)
            gated_a, s_new = _hgrn(proj, lb, vec(hgrn_norm), s0, batch, seq, _row_tile(seq, 256), GA_BLK)
            return gated_a, moba(proj, k_new, v_new), k_new, v_new, s_new

        s0p = jnp.zeros((bp, heads, HEAD_DIM, HEAD_DIM), F32)
        ga, gb, kp, vp, sp = mixers(xp, bp, tp, s0p,
                                    lambda pr, k, v: _moba_prompt(pr, k, v, bp, tp, Q_BLK, GB_BLK))
        xp, cp = _ffn_prompt(ga, gb, xp, w_out_b, vec(norm_mix_post), vec(norm_ffn_pre), w_up_b, conv_w[l],
                             conv_b[l].reshape(1, -1), w_down_b, vec(norm_ffn_post), bp, tp, _row_tile(tp, 512))
        ga, gb, kn, vn, sn = mixers(xs, bs, ts, state_hgrn[l],
                                    lambda pr, k, v: _moba_sample(pr, k, v, cache_k, cache_v, page_table, l, ts,
                                                                  Q_BLK, GB_BLK))
        x1, h2 = _merge(ga, gb, xs, w_out_b, vec(norm_mix_post), vec(norm_ffn_pre), _row_tile(bs * ts, 512))
        xs, cn = _ffn_sample(h2, x1, w_up_b, conv_w[l], conv_b[l].reshape(1, -1), w_down_b,
                             vec(norm_ffn_post), state_ffn_conv[l], ts)
        kv = lambda a, b, t: a.reshape(b, t, heads, HEAD_DIM)
        for o, val in zip(outs, (kv(kp, bp, tp), kv(vp, bp, tp), sp, cp, kv(kn, bs, ts), kv(vn, bs, ts), sn, cn)):
            o.append(val)
    return (xp.reshape(bp, tp, d), xs.reshape(bs, ts, d), *(jnp.stack(o) for o in outs))
```

```python
import functools
import math

import jax
import jax.numpy as jnp
from jax import lax
from jax.experimental import pallas as pl
from jax.experimental.pallas import tpu as pltpu

F32 = jnp.float32
BF16 = jnp.bfloat16

NORM_EPS = 1e-6
HEAD_DIM = 128
MOBA_BLOCK = 256
MOBA_TOPK = 3
HGRN_CHUNK = 64
LOOKAHEAD = 2
HGRN_SEQS_PER_STEP = 4
CONV_W = 3
BF16_ROWS = 16
SOFTMAX_LOG2_SCALE = math.log2(math.e) / math.sqrt(HEAD_DIM)
VMEM_LIMIT = 56 * 1024 * 1024


def _cparams(sem):
    return pltpu.CompilerParams(dimension_semantics=sem, vmem_limit_bytes=VMEM_LIMIT)


def _sigmoid(x):
    return 0.5 * jnp.tanh(0.5 * x) + 0.5


def _rms(x, gain):
    return x * lax.rsqrt(jnp.mean(x * x, axis=-1, keepdims=True) + NORM_EPS) * gain


def _dot(a, b):
    return jnp.dot(a, b, preferred_element_type=F32)


def _dot_nt(a, b):
    return lax.dot_general(a, b, (((1,), (1,)), ((), ())), preferred_element_type=F32)


def _split_bf16(x, parts):
    out = []
    for _ in range(parts):
        p = x.astype(BF16)
        out.append(p)
        x = x - p.astype(F32)
    return out


def _dot_nt_split(a, b, operand_dtype=BF16):
    ah, al = (p.astype(operand_dtype) for p in _split_bf16(a, 2))
    bh, bl = (p.astype(operand_dtype) for p in _split_bf16(b, 2))
    return _dot_nt(ah, bh) + _dot_nt(ah, bl) + _dot_nt(al, bh)


def _inproj_kernel(x_ref, g_ref, w_ref, p_ref, k_ref, v_ref, h_scr, *, main_steps, kv_steps):
    j = pl.program_id(1)

    @pl.when(j == 0)
    def _():
        h_scr[...] = _rms(x_ref[...], g_ref[...]).astype(BF16)

    @pl.when(j < main_steps)
    def _():
        p_ref[...] = _dot(h_scr[...], w_ref[...])

    @pl.when((j >= main_steps) & (j < main_steps + kv_steps))
    def _():
        k_ref[...] = _dot(h_scr[...], w_ref[...])

    @pl.when(j >= main_steps + kv_steps)
    def _():
        v_ref[...] = _dot(h_scr[...], w_ref[...])


def _inproj(x, gain, w_in, tm, tn):
    n, d = x.shape
    per = d // tn
    main_steps, kv_steps = 7 * per, per
    src_block = lambda jb: jnp.where(jb < 5, jb, jnp.where(jb < 7, jb + 2, jb - 2))
    return pl.pallas_call(
        functools.partial(_inproj_kernel, main_steps=main_steps, kv_steps=kv_steps),
        grid=(n // tm, 9 * per),
        in_specs=[
            pl.BlockSpec((tm, d), lambda i, j: (i, 0)),
            pl.BlockSpec((1, d), lambda i, j: (0, 0)),
            pl.BlockSpec((d, tn), lambda i, j: (0, src_block(j // per) * per + j % per)),
        ],
        out_specs=[
            pl.BlockSpec((tm, tn), lambda i, j: (i, jnp.minimum(j, main_steps - 1))),
            pl.BlockSpec((tm, tn), lambda i, j: (i, jnp.clip(j - main_steps, 0, kv_steps - 1))),
            pl.BlockSpec((tm, tn), lambda i, j: (i, jnp.clip(j - main_steps - kv_steps, 0, kv_steps - 1))),
        ],
        out_shape=[
            jax.ShapeDtypeStruct((n, 7 * d), F32),
            jax.ShapeDtypeStruct((n, d), F32),
            jax.ShapeDtypeStruct((n, d), F32),
        ],
        scratch_shapes=[pltpu.VMEM((tm, d), BF16)],
        compiler_params=_cparams(("parallel", "arbitrary")),
        name="inproj",
    )(x, gain, w_in)


def _hgrn_kernel(hq_ref, hf_ref, hi_ref, hg_ref, ga_ref, lb_ref, gain_ref, s0_ref, a_ref, sout_ref, st_scr,
                 *, heads, seqs):
    tt = pl.program_id(1)
    C = HGRN_CHUNK
    assert 2 * C == HEAD_DIM
    rows = hq_ref.shape[0] // seqs
    d = heads * HEAD_DIM
    n_chunks = -(-rows // C)
    pairs = [(s, h) for s in range(seqs) for h in range(heads)]
    lanes = lambda h: slice(h * HEAD_DIM, (h + 1) * HEAD_DIM)

    @pl.when(tt == 0)
    def _():
        for s, h in pairs:
            st_scr[s * heads + h] = s0_ref[s, h].T

    lb = lb_ref[...]
    gain = gain_ref[...]
    tril = jnp.where(lax.broadcasted_iota(jnp.int32, (C, C), 0) >= lax.broadcasted_iota(jnp.int32, (C, C), 1),
                     1.0, 0.0).astype(BF16)
    causal = (lax.broadcasted_iota(jnp.int32, (C, HEAD_DIM), 1) <= lax.broadcasted_iota(jnp.int32, (C, HEAD_DIM), 0))
    zero_rows = jnp.zeros((C, HEAD_DIM), BF16)
    states = {(s, h): st_scr[s * heads + h] for s, h in pairs}

    def chunk_operands(s, c):
        live = min(rows, (c + 1) * C) - c * C
        rs = slice(s * rows + c * C, s * rows + c * C + live)

        def padded(x, fill):
            return jnp.concatenate([x, jnp.full((C - live, d), fill, F32)], axis=0) if live < C else x

        f = padded(lb + (1.0 - lb) * _sigmoid(hf_ref[rs, :]), 1.0)
        hq = hq_ref[rs, :]
        q = padded(hq * _sigmoid(hq), 0.0)
        vc = padded(hi_ref[rs, :], 0.0)
        b = functools.reduce(jnp.add, [_dot(tril, part) for part in _split_bf16(jnp.log(f), 3)])
        decay = jnp.exp(b[C - 1:C])
        k_scaled = (1.0 - f) * jnp.exp(-b)
        qt = (q * jnp.exp(b)).astype(BF16)
        kh = (k_scaled * decay).astype(BF16)
        return rs, live, decay, qt, k_scaled.astype(BF16), kh, vc, vc.astype(BF16)

    ahead = [chunk_operands(s, 0) for s in range(seqs)]
    for c in range(n_chunks):
        cur = ahead
        if c + 1 < n_chunks:
            ahead = [chunk_operands(s, c + 1) for s in range(seqs)]
        decay, qt, kt, kh, vc, vb = ({s: cur[s][i] for s in range(seqs)} for i in range(2, 8))
        out1 = {(s, h): _dot_nt(qt[s][:, lanes(h)], jnp.concatenate(
            [states[s, h].astype(BF16), kt[s][:, lanes(h)], zero_rows], axis=0)) for s, h in pairs}
        v_t = {(s, h): jnp.concatenate([jnp.zeros((C, HEAD_DIM), F32), vc[s][:, lanes(h)]], axis=0).T.astype(BF16)
               for s, h in pairs}
        scores = {p: jnp.where(causal, out1[p][:, HEAD_DIM:], 0.0).astype(BF16) for p in pairs}
        out2 = {(s, h): _dot(jnp.concatenate([scores[s, h], v_t[s, h]], axis=0),
                             jnp.concatenate([vb[s][:, lanes(h)], kh[s][:, lanes(h)]], axis=0)) for s, h in pairs}
        states = {(s, h): states[s, h] * decay[s][:, lanes(h)] + out2[s, h][C:] for s, h in pairs}
        for s in range(seqs):
            rs, live = cur[s][:2]
            normed = jnp.concatenate([_rms(out1[s, h][:, :HEAD_DIM] + out2[s, h][:C], gain[:, lanes(h)])
                                      for h in range(heads)], axis=1)
            hg = hg_ref[rs, :]
            a_ref[rs, :] = _sigmoid(ga_ref[rs, :]) * (normed[:live] * (hg * _sigmoid(hg)))

    for s, h in pairs:
        st_scr[s * heads + h] = states[s, h]

    @pl.when(tt == pl.num_programs(1) - 1)
    def _():
        for s, h in pairs:
            sout_ref[s, h] = st_scr[s * heads + h].T


def _hgrn(proj, lb, gain, s0, batch, seq, t_tile, ga_block):
    n = proj.shape[0]
    d = lb.shape[1]
    heads = d // HEAD_DIM
    n_t = seq // t_tile
    seqs = math.gcd(batch, HGRN_SEQS_PER_STEP) if n_t == 1 and t_tile < HGRN_CHUNK else 1
    col_spec = lambda cb: pl.BlockSpec((seqs * t_tile, d), lambda b, t: (b * n_t + t, cb))
    state_spec = pl.BlockSpec((seqs, heads, HEAD_DIM, HEAD_DIM), lambda b, t: (b, 0, 0, 0))
    return pl.pallas_call(
        functools.partial(_hgrn_kernel, heads=heads, seqs=seqs),
        grid=(batch // seqs, n_t),
        in_specs=[col_spec(0), col_spec(1), col_spec(2), col_spec(3), col_spec(ga_block),
                  pl.BlockSpec((1, d), lambda b, t: (0, 0)),
                  pl.BlockSpec((1, d), lambda b, t: (0, 0)),
                  state_spec],
        out_specs=[col_spec(0), state_spec],
        out_shape=[jax.ShapeDtypeStruct((n, d), F32),
                   jax.ShapeDtypeStruct((batch, heads, HEAD_DIM, HEAD_DIM), F32)],
        scratch_shapes=[pltpu.VMEM((seqs * heads, HEAD_DIM, HEAD_DIM), F32)],
        compiler_params=_cparams(("parallel", "arbitrary")),
        name="hgrn",
    )(proj, proj, proj, proj, proj, lb, gain, s0)


def _topk_bias(s, valid, topk):
    n_blocks = len(valid)
    rows = []
    for n in range(n_blocks):
        sn = s[n:n + 1, :]
        cnt = jnp.zeros(sn.shape, jnp.int32)
        for m in range(n_blocks):
            if m == n:
                continue
            sm = s[m:m + 1, :]
            beats = (sm >= sn) if m < n else (sm > sn)
            cnt = cnt + jnp.where(beats & valid[m], 1, 0)
        rows.append(jnp.where(cnt < topk, 0.0, -jnp.inf))
    return rows


def _moba_prompt_kernel(q_ref, k_ref, v_ref, gb_ref, o_ref, *, nb, blk):
    t = nb * blk
    q = q_ref[...]
    k = k_ref[...]
    qb = (q * SOFTMAX_LOG2_SCALE).astype(BF16)
    kb = k.astype(BF16)
    vt = v_ref[...].T.astype(BF16)

    kmean = jnp.concatenate(
        [jnp.mean(k[n * blk:(n + 1) * blk, :], axis=0, keepdims=True) for n in range(nb)]
        + [jnp.zeros((BF16_ROWS - nb % BF16_ROWS, HEAD_DIM), F32)], axis=0)
    s = _dot_nt_split(kmean, q)
    q_blk = lax.broadcasted_iota(jnp.int32, (1, t), 1) // blk
    bias = _topk_bias(s, [q_blk > m for m in range(nb - 1)], MOBA_TOPK)

    key = lax.broadcasted_iota(jnp.int32, (blk, blk), 0)
    qry = lax.broadcasted_iota(jnp.int32, (blk, blk), 1)
    logits = lambda j: _dot_nt(kb[:(j + 1) * blk, :], qb[j * blk:(j + 1) * blk, :])
    lg_next = logits(nb - 1)
    for j in reversed(range(nb)):
        qs = slice(j * blk, (j + 1) * blk)
        lg = lg_next
        if j > 0:
            lg_next = logits(j - 1)
        parts = [lg[n * blk:(n + 1) * blk, :] + bias[n][:, qs] for n in range(j)]
        parts.append(jnp.where(key <= qry, lg[j * blk:, :], -jnp.inf))
        m = functools.reduce(jnp.maximum, [jnp.max(p, axis=0, keepdims=True) for p in parts])
        ps = [jnp.exp2(p - m) for p in parts]
        l = functools.reduce(jnp.add, [jnp.sum(p, axis=0, keepdims=True) for p in ps])
        pb = jnp.concatenate([p.astype(BF16) for p in ps], axis=0)
        o_ref[qs, :] = _sigmoid(gb_ref[qs, :]) * (_dot(vt[:, :(j + 1) * blk], pb) / l).T


def _moba_prompt(proj, k_new, v_new, batch, seq, q_col_block, gb_col_block):
    n, d = k_new.shape
    heads = d // HEAD_DIM
    blk = MOBA_BLOCK
    nb = seq // blk
    seq_head = lambda cb: pl.BlockSpec((seq, HEAD_DIM), lambda b, h: (b, cb * heads + h))
    return pl.pallas_call(
        functools.partial(_moba_prompt_kernel, nb=nb, blk=blk),
        grid=(batch, heads),
        in_specs=[seq_head(q_col_block), seq_head(0), seq_head(0), seq_head(gb_col_block)],
        out_specs=seq_head(0),
        out_shape=jax.ShapeDtypeStruct((n, d), F32),
        compiler_params=_cparams(("parallel", "parallel")),
        name="moba_prompt",
    )(proj, k_new, v_new, proj)


def _block_diag_queries(q, heads, rows):
    d = q.shape[1]
    tiled = jnp.concatenate([q] * heads, axis=0)
    r = lax.broadcasted_iota(jnp.int32, (heads * rows, d), 0) // rows
    c = lax.broadcasted_iota(jnp.int32, (heads * rows, d), 1) // HEAD_DIM
    return jnp.where(r == c, tiled, 0.0)


def _moba_sample_kernel(pt_ref, q_ref, kn_ref, vn_ref, gb_ref, ck_hbm, cv_hbm, o_ref, lg_scr, ksum_scr, kmean_scr,
                        bias_scr, hmask_scr, m_scr, l_scr, acc_scr, page_buf, page_sem,
                        *, heads, rows, page, pages_per_step, layer, n_pages):
    phase, g = pl.program_id(1), pl.program_id(2)
    steps = pl.num_programs(2)
    step = (pl.program_id(0) * 2 + phase) * steps + g
    n_steps = pl.num_programs(0) * 2 * steps
    ring = page_buf.shape[0]

    def group_copies(at_step, wait):
        b, rest = at_step // (2 * steps), at_step % (2 * steps)
        ph, gg = rest // steps, rest % steps
        slot = at_step % ring
        for src, when in ((ck_hbm, ph == 0), (cv_hbm, ph == 1)):
            @pl.when(when)
            def _():
                for i in range(pages_per_step):
                    cp = pltpu.make_async_copy(src.at[layer, pt_ref[b * n_pages + gg * pages_per_step + i]],
                                               page_buf.at[slot, i], page_sem.at[slot, i])
                    cp.wait() if wait else cp.start()

    @pl.when(step == 0)
    def _():
        for ahead in range(LOOKAHEAD):
            group_copies(step + ahead, wait=False)

    @pl.when(step + LOOKAHEAD < n_steps)
    def _():
        group_copies(step + LOOKAHEAD, wait=False)

    group_copies(step, wait=True)
    slot = step % ring
    k_refs = v_refs = [page_buf.at[slot, i] for i in range(pages_per_step)]
    blocks = range(pages_per_step // 2)
    hr = heads * rows
    d = heads * HEAD_DIM
    nb = bias_scr.shape[0]
    head = lambda h: slice(h * HEAD_DIM, (h + 1) * HEAD_DIM)
    pos_head = lambda ref: ref[...].reshape(page * heads, HEAD_DIM).astype(BF16)
    side_by_side = lambda refs, i: jnp.concatenate([pos_head(refs[2 * i]), pos_head(refs[2 * i + 1])], axis=1)

    q = q_ref[...]
    qs = q * SOFTMAX_LOG2_SCALE

    @pl.when(phase == 0)
    def _():
        for i, k_ref in enumerate(k_refs):
            ksum_scr[g * pages_per_step + i] = jnp.sum(k_ref[...], axis=0)
        q_hd = jnp.concatenate([qs[:, head(h)] for h in range(heads)], axis=0).astype(BF16)
        zero = jnp.zeros_like(q_hd)
        q_pair = jnp.concatenate([jnp.concatenate([q_hd, zero], axis=1),
                                  jnp.concatenate([zero, q_hd], axis=1)], axis=0)
        for i in blocks:
            lg_scr[g * len(blocks) + i] = _dot_nt(q_pair, side_by_side(k_refs, i))

    @pl.when((phase == 1) & (g == 0))
    def _():
        for n in range(nb):
            kmean_scr[n * heads:(n + 1) * heads, :] = (ksum_scr[2 * n] + ksum_scr[2 * n + 1]) * (0.5 / page)
        s = jnp.concatenate([_dot_nt_split(q[:, head(h)], kmean_scr[pl.ds(h, nb, stride=heads), :], F32)
                             for h in range(heads)], axis=0)
        cnt = jnp.zeros((hr, nb), jnp.int32)
        idx = lax.broadcasted_iota(jnp.int32, (hr, nb), 1)
        for m in range(nb):
            sm = s[:, m:m + 1]
            beats = (sm > s) | ((sm == s) & (idx > m))
            cnt = cnt + jnp.where(beats, 1, 0)
        bias = jnp.where(cnt < MOBA_TOPK, 0.0, -jnp.inf)
        for n in range(nb):
            col = jnp.broadcast_to(bias[:, n:n + 1], (hr, HEAD_DIM))
            bias_scr[n] = jnp.concatenate([col, col], axis=0)
        r = lax.broadcasted_iota(jnp.int32, hmask_scr.shape, 0)
        c = lax.broadcasted_iota(jnp.int32, hmask_scr.shape, 1)
        hmask_scr[...] = jnp.where(c % heads == (r % hr) // rows, 0.0, -jnp.inf)

        qbd = _block_diag_queries(qs, heads, rows).astype(BF16)
        kn = jnp.concatenate([kn_ref[...], jnp.zeros((page - rows, d), F32)], axis=0)
        vn = jnp.concatenate([vn_ref[...], jnp.zeros((page - rows, d), F32)], axis=0)
        r_own = lax.broadcasted_iota(jnp.int32, (hr, page), 0) % rows
        c_own = lax.broadcasted_iota(jnp.int32, (hr, page), 1)
        lg_own = jnp.where(c_own <= r_own, _dot_nt(qbd, kn.astype(BF16)), -jnp.inf)
        m0 = jnp.max(lg_own, axis=1, keepdims=True)
        p_own = jnp.exp2(lg_own - m0)
        pv = _dot(p_own.astype(BF16), vn.astype(BF16))
        m_scr[...] = m0
        l_scr[...] = jnp.sum(p_own, axis=1, keepdims=True)
        acc_scr[...] = jnp.concatenate([pv[h * rows:(h + 1) * rows, head(h)] for h in range(heads)], axis=0)

    @pl.when(phase == 1)
    def _():
        hmask = hmask_scr[...]
        lgs = [lg_scr[g * len(blocks) + i] + hmask
               + jnp.concatenate([bias_scr[g * len(blocks) + i]] * heads, axis=1) for i in blocks]
        m_row = functools.reduce(jnp.maximum, [jnp.max(lg, axis=1, keepdims=True) for lg in lgs])
        m_old = m_scr[...]
        m_new = jnp.maximum(m_old, jnp.maximum(m_row[:hr], m_row[hr:]))
        alpha = jnp.exp2(m_old - m_new)
        m_pair = jnp.concatenate([m_new, m_new], axis=0)
        ps = [jnp.exp2(lg - m_pair) for lg in lgs]
        l_row = functools.reduce(jnp.add, [jnp.sum(p, axis=1, keepdims=True) for p in ps])
        pv = _dot(jnp.concatenate([p.astype(BF16) for p in ps], axis=1),
                  jnp.concatenate([side_by_side(v_refs, i) for i in blocks], axis=0))
        m_scr[...] = m_new
        l_scr[...] = alpha * l_scr[...] + l_row[:hr] + l_row[hr:]
        acc_scr[...] = alpha * acc_scr[...] + pv[:hr, :HEAD_DIM] + pv[hr:, HEAD_DIM:]

    @pl.when((phase == 1) & (g == pl.num_programs(2) - 1))
    def _():
        out = acc_scr[...] / l_scr[...]
        for h in range(heads):
            o_ref[:, head(h)] = _sigmoid(gb_ref[:, head(h)]) * out[h * rows:(h + 1) * rows, :]


def _moba_sample(proj, k_new, v_new, cache_k, cache_v, page_table, layer, rows, q_col_block, gb_col_block):
    n, d = k_new.shape
    _, _, page, heads, dh = cache_k.shape
    batch, n_pages = page_table.shape
    assert dh == HEAD_DIM and 2 * page == MOBA_BLOCK and n_pages % 2 == 0
    nb = n_pages // 2
    hr = heads * rows
    pt = page_table.reshape(-1)

    pps = 2 * math.gcd(nb, 4)
    steps = n_pages // pps

    row_spec = lambda cb: pl.BlockSpec((rows, d), lambda b, ph, g, pt: (b, cb))
    return pl.pallas_call(
        functools.partial(_moba_sample_kernel, heads=heads, rows=rows, page=page, pages_per_step=pps,
                          layer=layer, n_pages=n_pages),
        grid_spec=pltpu.PrefetchScalarGridSpec(
            num_scalar_prefetch=1,
            grid=(batch, 2, steps),
            in_specs=[row_spec(q_col_block), row_spec(0), row_spec(0), row_spec(gb_col_block),
                      pl.BlockSpec(memory_space=pl.ANY), pl.BlockSpec(memory_space=pl.ANY)],
            out_specs=row_spec(0),
            scratch_shapes=[pltpu.VMEM((nb, 2 * hr, page * heads), F32),
                            pltpu.VMEM((n_pages, heads, dh), F32),
                            pltpu.VMEM((nb * heads, dh), F32),
                            pltpu.VMEM((nb, 2 * hr, dh), F32),
                            pltpu.VMEM((2 * hr, page * heads), F32),
                            pltpu.VMEM((hr, 1), F32), pltpu.VMEM((hr, 1), F32), pltpu.VMEM((hr, dh), F32),
                            pltpu.VMEM((LOOKAHEAD + 1, pps, page, heads, dh), F32),
                            pltpu.SemaphoreType.DMA((LOOKAHEAD + 1, pps))],
        ),
        out_shape=jax.ShapeDtypeStruct((n, d), F32),
        compiler_params=_cparams(("arbitrary", "arbitrary", "arbitrary")),
        name="moba_sample",
    )(pt, proj, k_new, v_new, proj, cache_k, cache_v)


def _merge_rows(gated_a, gated_b, x, w_out, g_post, g_pre):
    m = _dot((gated_a + gated_b).astype(BF16), w_out)
    x1 = x + _rms(m, g_post)
    return x1, _rms(x1, g_pre).astype(BF16)


def _merge_kernel(a_ref, b_ref, x_ref, w_ref, gpost_ref, gpre_ref, x1_ref, h2_ref):
    x1_ref[...], h2_ref[...] = _merge_rows(a_ref[...], b_ref[...], x_ref[...], w_ref[...],
                                           gpost_ref[...], gpre_ref[...])


def _merge(gated_a, gated_b, x, w_out, g_post, g_pre, tm):
    n, d = x.shape
    row = pl.BlockSpec((tm, d), lambda i: (i, 0))
    vec = pl.BlockSpec((1, d), lambda i: (0, 0))
    return pl.pallas_call(
        _merge_kernel,
        grid=(n // tm,),
        in_specs=[row, row, row, pl.BlockSpec((d, d), lambda i: (0, 0)), vec, vec],
        out_specs=[row, row],
        out_shape=[jax.ShapeDtypeStruct((n, d), F32), jax.ShapeDtypeStruct((n, d), BF16)],
        compiler_params=_cparams(("parallel",)),
        name="merge",
    )(gated_a, gated_b, x, w_out, g_post, g_pre)


def _gelu_tanh(x):
    return 0.5 * x * (1.0 + jnp.tanh(math.sqrt(2.0 / math.pi) * (x + 0.044715 * (x * x * x))))


def _ffn_prompt_kernel(a_ref, b_ref, x_ref, wout_ref, gpost_ref, gpre_ref, wup_ref, cw_ref, cb_ref, wdn_ref,
                       g_ref, y_ref, st_ref, carry_scr, win_scr, act_scr, *, d_ff, cs):
    tm = x_ref.shape[0]
    x1, h = _merge_rows(a_ref[...], b_ref[...], x_ref[...], wout_ref[...], gpost_ref[...], gpre_ref[...])
    halo = carry_scr.shape[0]

    @pl.when(pl.program_id(1) == 0)
    def _():
        carry_scr[...] = jnp.zeros_like(carry_scr)

    def conv(col0, win):
        cols = slice(col0, col0 + cs)
        up = _dot(h, wup_ref[:, cols])
        win[0:halo, :] = carry_scr[:, cols]
        win[halo:, :] = up
        carry_scr[:, cols] = up[tm - halo:, :]
        st_ref[0, :, cols] = up[tm - (CONV_W - 1):, :]
        cw = cw_ref[:, cols]
        return (cb_ref[:, cols] + cw[0:1, :] * win[halo - 2:halo - 2 + tm, :]
                + cw[1:2, :] * win[halo - 1:halo - 1 + tm, :] + cw[2:3, :] * up)

    for c in range(d_ff // cs):
        gate = conv(c * cs, win_scr.at[(2 * c) % win_scr.shape[0]])
        lin = conv(d_ff + c * cs, win_scr.at[(2 * c + 1) % win_scr.shape[0]])
        act_scr[:, c * cs:(c + 1) * cs] = (_gelu_tanh(gate) * lin).astype(BF16)
    y_ref[...] = x1 + _rms(_dot(act_scr[...], wdn_ref[...]), g_ref[...])


def _ffn_sample_kernel(h_ref, x_ref, wg_ref, wl_ref, cwg_ref, cwl_ref, cbg_ref, cbl_ref, wdn_ref, g_ref,
                       sg_ref, sl_ref, y_ref, ng_ref, nl_ref, acc_scr):
    c = pl.program_id(0)
    h = h_ref[...]
    batch, _, cs = sg_ref.shape
    seq_rows = h.shape[0] // batch
    t = lax.broadcasted_iota(jnp.int32, (batch, seq_rows, cs), 1)

    @pl.when(c == 0)
    def _():
        acc_scr[...] = jnp.zeros_like(acc_scr)

    def conv(w_ref, s_ref, n_ref, cw_ref, cb_ref):
        up = _dot(h, w_ref[...])
        by_seq = lambda a: a.reshape(batch, seq_rows, cs)
        n_ref[...] = by_seq(up)[:, seq_rows - (CONV_W - 1):, :]
        prev2, prev1 = s_ref[:, 0:1, :], s_ref[:, 1:2, :]
        p1 = jnp.where(t == 0, prev1, by_seq(pltpu.roll(up, 1, 0)))
        p2 = jnp.where(t == 0, prev2, jnp.where(t == 1, prev1, by_seq(pltpu.roll(up, 2, 0))))
        cw = cw_ref[...]
        return (cb_ref[...] + cw[0:1, :] * p2 + cw[1:2, :] * p1 + cw[2:3, :] * by_seq(up)).reshape(up.shape)

    gate = conv(wg_ref, sg_ref, ng_ref, cwg_ref, cbg_ref)
    lin = conv(wl_ref, sl_ref, nl_ref, cwl_ref, cbl_ref)
    acc_scr[...] += _dot((_gelu_tanh(gate) * lin).astype(BF16), wdn_ref[...])

    @pl.when(c == pl.num_programs(0) - 1)
    def _():
        y_ref[...] = x_ref[...] + _rms(acc_scr[...], g_ref[...])


def _ffn_cs(d_ff):
    for cs in (512, 256, 128):
        if d_ff % cs == 0:
            return cs
    raise ValueError(f"d_ff={d_ff} is not a multiple of the 128-lane tile")


def _ffn_prompt(gated_a, gated_b, x, w_out, g_post, g_pre, w_up, conv_w, conv_b, w_down, gain, batch, seq, tm):
    n, d = x.shape
    d_ff = w_down.shape[0]
    n_t = seq // tm
    full = lambda a: pl.BlockSpec(a.shape, lambda b, t: (0, 0), pipeline_mode=pl.Buffered(1))
    row = pl.BlockSpec((tm, d), lambda b, t: (b * n_t + t, 0))
    cs = _ffn_cs(d_ff)
    halo = 8
    return pl.pallas_call(
        functools.partial(_ffn_prompt_kernel, d_ff=d_ff, cs=cs),
        grid=(batch, n_t),
        in_specs=[row, row, row, full(w_out), full(g_post), full(g_pre),
                  full(w_up), full(conv_w), full(conv_b), full(w_down), full(gain)],
        out_specs=[row, pl.BlockSpec((1, CONV_W - 1, 2 * d_ff), lambda b, t: (b, 0, 0))],
        out_shape=[jax.ShapeDtypeStruct((n, d), F32),
                   jax.ShapeDtypeStruct((batch, CONV_W - 1, 2 * d_ff), F32)],
        scratch_shapes=[pltpu.VMEM((halo, 2 * d_ff), F32), pltpu.VMEM((4, halo + tm, cs), F32),
                        pltpu.VMEM((tm, d_ff), BF16)],
        compiler_params=_cparams(("parallel", "arbitrary")),
        name="ffn_prompt",
    )(gated_a, gated_b, x, w_out, g_post, g_pre, w_up, conv_w, conv_b, w_down, gain)


def _ffn_sample(h2, x1, w_up, conv_w, conv_b, w_down, gain, state, seq):
    n, d = x1.shape
    d_ff = w_down.shape[0]
    batch = n // seq
    cs = _ffn_cs(d_ff)
    nc = d_ff // cs
    full = lambda a: pl.BlockSpec(a.shape, lambda c: (0, 0))
    gate = lambda rows: pl.BlockSpec((rows, cs), lambda c: (0, c))
    lin = lambda rows: pl.BlockSpec((rows, cs), lambda c: (0, nc + c))
    st_gate = pl.BlockSpec((batch, CONV_W - 1, cs), lambda c: (0, 0, c))
    st_lin = pl.BlockSpec((batch, CONV_W - 1, cs), lambda c: (0, 0, nc + c))
    half_state = jax.ShapeDtypeStruct((batch, CONV_W - 1, d_ff), F32)
    y, new_gate, new_lin = pl.pallas_call(
        _ffn_sample_kernel,
        grid=(nc,),
        in_specs=[full(h2), full(x1), gate(d), lin(d), gate(CONV_W), lin(CONV_W), gate(1), lin(1),
                  pl.BlockSpec((cs, d), lambda c: (c, 0)), full(gain), st_gate, st_lin],
        out_specs=[full(x1), st_gate, st_gate],
        out_shape=[jax.ShapeDtypeStruct((n, d), F32), half_state, half_state],
        scratch_shapes=[pltpu.VMEM((n, d), F32)],
        compiler_params=_cparams(("arbitrary",)),
        name="ffn_sample",
    )(h2, x1, w_up, w_up, conv_w, conv_w, conv_b, conv_b, w_down, gain, state, state)
    return y, jnp.concatenate([new_gate, new_lin], axis=-1)


def _row_tile(n, cap):
    t = min(n, cap)
    while n % t:
        t //= 2
    return t


def kernel(x_prompt, x_sample, cache_k, cache_v, page_table, state_hgrn, state_ffn_conv, norm_mix_pre, norm_mix_post, w_in, hgrn_lb_logits, hgrn_norm, w_out, norm_ffn_pre, norm_ffn_post, w_up, conv_w, conv_b, w_down):
    bp, tp, d = x_prompt.shape
    bs, ts, _ = x_sample.shape
    depth = w_in.shape[0]
    heads = d // HEAD_DIM
    lb_all = jnp.cumsum(jax.nn.softmax(hgrn_lb_logits.astype(F32), axis=0), axis=0)

    xp = x_prompt.reshape(bp * tp, d)
    xs = x_sample.reshape(bs * ts, d)
    outs = [[] for _ in range(8)]
    for l in range(depth):
        w_in_b = w_in[l].astype(BF16)
        w_out_b = w_out[l].astype(BF16)
        w_up_b = w_up[l].astype(BF16)
        w_down_b = w_down[l].astype(BF16)
        vec = lambda a: a[l].reshape(1, -1)
        lb = lb_all[l].reshape(1, d)
        Q_BLK, GA_BLK, GB_BLK = 4, 5, 6

        def mixers(x, batch, seq, s0, moba):
            proj, k_new, v_new = _inproj(x, vec(norm_mix_pre), w_in_b, _row_tile(x.shape[0], 1024), d)
            gated_a, s_new = _hgrn(proj, lb, vec(hgrn_norm), s0, batch, seq, _row_tile(seq, 256), GA_BLK)
            return gated_a, moba(proj, k_new, v_new), k_new, v_new, s_new

        s0p = jnp.zeros((bp, heads, HEAD_DIM, HEAD_DIM), F32)
        ga, gb, kp, vp, sp = mixers(xp, bp, tp, s0p,
                                    lambda pr, k, v: _moba_prompt(pr, k, v, bp, tp, Q_BLK, GB_BLK))
        xp, cp = _ffn_prompt(ga, gb, xp, w_out_b, vec(norm_mix_post), vec(norm_ffn_pre), w_up_b, conv_w[l],
                             conv_b[l].reshape(1, -1), w_down_b, vec(norm_ffn_post), bp, tp, _row_tile(tp, 512))
        ga, gb, kn, vn, sn = mixers(xs, bs, ts, state_hgrn[l],
                                    lambda pr, k, v: _moba_sample(pr, k, v, cache_k, cache_v, page_table, l, ts,
                                                                  Q_BLK, GB_BLK))
        x1, h2 = _merge(ga, gb, xs, w_out_b, vec(norm_mix_post), vec(norm_ffn_pre), _row_tile(bs * ts, 512))
        xs, cn = _ffn_sample(h2, x1, w_up_b, conv_w[l], conv_b[l].reshape(1, -1), w_down_b,
                             vec(norm_ffn_post), state_ffn_conv[l], ts)
        kv = lambda a, b, t: a.reshape(b, t, heads, HEAD_DIM)
        for o, val in zip(outs, (kv(kp, bp, tp), kv(vp, bp, tp), sp, cp, kv(kn, bs, ts), kv(vn, bs, ts), sn, cn)):
            o.append(val)
    return (xp.reshape(bp, tp, d), xs.reshape(bs, ts, d), *(jnp.stack(o) for o in outs))
```
